```python
import math
import jax, jax.numpy as jnp
from jax import lax
import numpy as np

D_MODEL = 1024
BATCH = 16
SEQ = 2048
DEPTH = 2
DEC_BATCH = 8
DEC_SEQ = 2048
PAST_LEN = 128

GRID_W = 64
D_FF = 2816
N_EVEN = (DEPTH + 1) // 2
N_ODD = DEPTH // 2
EPS = 1e-6
POOL_WIDTH = D_MODEL // 2
POOL_GROUPS = 4
POOL_GROUP_W = POOL_WIDTH // POOL_GROUPS
POOL_WINDOWS = (2, 4, 8, 16)
HYENA_WIDTH = D_MODEL // 2
HYENA_BANDS = 16
HYENA_EMB = 1 + 2 * HYENA_BANDS
HYENA_FILTER_HIDDEN = 64
HYENA_SHORT = 3
HYENA_MOD_SHIFT = 0.05
HYENA_DECAY_TARGET = 1e-2
HYENA_FAST_DECAY_PCT = 0.3
HYENA_SLOW_DECAY_PCT = 1.5
AB_IN = POOL_WIDTH + 3 * HYENA_WIDTH
AB_OUT = POOL_WIDTH + HYENA_WIDTH
N_HEADS = 16
HEAD_DIM = D_MODEL // N_HEADS
NA_MAX_ROWS = 8
NA_COLS = 16

kernel_name = "hybrid_pool_hyena_natten_encoder"


def rmsnorm(x, g):
    xf = x.astype(jnp.float32)
    y = xf * lax.rsqrt(jnp.mean(xf * xf, axis=-1, keepdims=True) + EPS)
    return (y * g.astype(jnp.float32)).astype(x.dtype)


def swiglu(x, wg, wu, wd):
    return (jax.nn.silu(x @ wg) * (x @ wu)) @ wd


def pool_mixer(u, w_grp, scale):
    B, L, _ = u.shape
    t = np.arange(L)[:, None]
    w = np.array(POOL_WINDOWS)[None, :]
    lo = np.clip(t - w // 2, 0, L - 1)
    hi = np.clip(t + w - w // 2 - 1, 0, L - 1)
    cnt = (hi - lo + 1).astype(np.float32)
    gidx = np.arange(POOL_GROUPS)[None, :]
    uf = u.astype(jnp.float32).reshape(B, L, POOL_GROUPS, POOL_GROUP_W)
    cs = jnp.concatenate([jnp.zeros((B, 1, POOL_GROUPS, POOL_GROUP_W), jnp.float32),
                          jnp.cumsum(uf, axis=1)], axis=1)
    win_sum = cs[:, hi + 1, gidx, :] - cs[:, lo, gidx, :]
    pooled = (win_sum / cnt[None, :, :, None] - uf).astype(u.dtype)
    mixed = jnp.einsum('blgc,gcd->blgd', pooled, w_grp)
    return mixed.reshape(B, L, POOL_WIDTH) * scale


def short_conv(u, w, b):
    up = jnp.pad(u, ((0, 0), (1, 1), (0, 0)))
    return up[:, :-2] * w[0] + up[:, 1:-1] * w[1] + up[:, 2:] * w[2] + b


def hyena_filter_fft(L, w1, b1, w2, b2, w3, b3, freq, w_out, deltas):
    t_idx = jnp.arange(L, dtype=jnp.float32)
    t_norm = jnp.linspace(0.0, 1.0, L, dtype=jnp.float32)
    bands = jnp.linspace(1e-4, HYENA_BANDS - 1, HYENA_BANDS, dtype=jnp.float32)
    ang = (2.0 * math.pi / L) * t_idx[:, None] * bands[None, :]
    z = jnp.concatenate([t_norm[:, None], jnp.cos(ang), -jnp.sin(ang)], axis=-1)
    h = jnp.sin(freq * (z @ w1 + b1))
    h = jnp.sin(freq * (h @ w2 + b2))
    h = jnp.sin(freq * (h @ w3 + b3))
    h = (h @ w_out).astype(jnp.float32)
    decay = jnp.exp(-t_norm[:, None] * jnp.abs(deltas.astype(jnp.float32)))
    h = (h * (decay + HYENA_MOD_SHIFT)).reshape(L, 2, HYENA_WIDTH)
    k = jnp.concatenate([h[:, 0], jnp.zeros((1, HYENA_WIDTH), jnp.float32), h[:0:-1, 1]], axis=0)
    return jnp.fft.rfft(k, axis=0)


def fftconv(u, k_f, bias):
    L = u.shape[1]
    uf = u.astype(jnp.float32)
    y = jnp.fft.irfft(jnp.fft.rfft(uf, n=2 * L, axis=1) * k_f[None], n=2 * L, axis=1)[:, :L]
    return y + uf * bias.astype(jnp.float32)


def pool_hyena_mixer(xn, w_in, pool_w, pool_scale, short_w, short_b, fw1, fb1, fw2, fb2, fw3, fb3,
                     ffreq, fw_out, fdeltas, h_bias, w_out):
    L = xn.shape[1]
    p = xn @ w_in
    a = pool_mixer(p[..., :POOL_WIDTH], pool_w, pool_scale)
    hb = short_conv(p[..., POOL_WIDTH:], short_w, short_b)
    x0 = hb[..., :HYENA_WIDTH]
    x1 = hb[..., HYENA_WIDTH:2 * HYENA_WIDTH]
    v = hb[..., 2 * HYENA_WIDTH:]
    k_f = hyena_filter_fft(L, fw1, fb1, fw2, fb2, fw3, fb3, ffreq, fw_out, fdeltas)
    yh = (fftconv(v * x1, k_f, h_bias) * x0.astype(jnp.float32)).astype(xn.dtype)
    return jnp.concatenate([a, yh], axis=-1) @ w_out


def na_mixer(xn, w_qkv, gq, gk, rpb, w_o):
    B, L, _ = xn.shape
    rows = L // GRID_W
    kh = min(NA_MAX_ROWS, rows)
    qkv = (xn @ w_qkv).reshape(B, L, 3, N_HEADS, HEAD_DIM)
    q = rmsnorm(qkv[:, :, 0], gq)
    k = rmsnorm(qkv[:, :, 1], gk)
    v = qkv[:, :, 2]

    def to_grid(a):
        return a.reshape(B, rows, GRID_W, N_HEADS, HEAD_DIM).transpose(0, 3, 1, 2, 4)

    qg, kg, vg = to_grid(q), to_grid(k), to_grid(v)
    c = np.arange(GRID_W)
    c_start = np.clip(c - NA_COLS // 2, 0, GRID_W - NA_COLS)
    col_mask = (c[None, :] >= c_start[:, None]) & (c[None, :] < c_start[:, None] + NA_COLS)
    dc_idx = np.clip(c[None, :] - c[:, None] + NA_COLS - 1, 0, 2 * NA_COLS - 2)
    rpb_cols = rpb[:, :, dc_idx].astype(jnp.float32)
    scale = HEAD_DIM ** -0.5

    def one_row(r):
        s = jnp.clip(r - kh // 2, 0, rows - kh)
        k_blk = lax.dynamic_slice_in_dim(kg, s, kh, axis=2)
        v_blk = lax.dynamic_slice_in_dim(vg, s, kh, axis=2)
        q_r = lax.dynamic_index_in_dim(qg, r, axis=2, keepdims=False)
        sc = jnp.einsum('bhqd,bhikd->bhqik', q_r, k_blk, preferred_element_type=jnp.float32) * scale
        dr = s + jnp.arange(kh) - r + NA_MAX_ROWS - 1
        bias = jnp.take(rpb_cols, dr, axis=1).transpose(0, 2, 1, 3)
        sc = jnp.where(col_mask[:, None, :], sc + bias, -jnp.inf)
        p = jax.nn.softmax(sc, axis=(-2, -1))
        return jnp.einsum('bhqik,bhikd->bhqd', p.astype(v_blk.dtype), v_blk)

    out = lax.map(one_row, jnp.arange(rows))
    out = out.transpose(1, 0, 3, 2, 4).reshape(B, L, D_MODEL)
    return out @ w_o


def run_trunk(x, norm_g, ffn_w_gate, ffn_w_up, ffn_w_down, w_in_ab, pool_w, pool_scale, short_w, short_b,
              filt_w1, filt_b1, filt_w2, filt_b2, filt_w3, filt_b3, filt_freq, filt_w_out, filt_deltas,
              hyena_bias, w_out_ab, w_qkv, q_norm_g, k_norm_g, rpb, w_o):
    for layer in range(DEPTH):
        x = x + 0.5 * swiglu(rmsnorm(x, norm_g[layer, 0]), ffn_w_gate[layer, 0], ffn_w_up[layer, 0], ffn_w_down[layer, 0])
        xn = rmsnorm(x, norm_g[layer, 1])
        i = layer // 2
        if layer % 2 == 0:
            mix = pool_hyena_mixer(xn, w_in_ab[i], pool_w[i], pool_scale[i], short_w[i], short_b[i],
                                   filt_w1[i], filt_b1[i], filt_w2[i], filt_b2[i], filt_w3[i], filt_b3[i],
                                   filt_freq[i], filt_w_out[i], filt_deltas[i], hyena_bias[i], w_out_ab[i])
        else:
            mix = na_mixer(xn, w_qkv[i], q_norm_g[i], k_norm_g[i], rpb[i], w_o[i])
        x = x + mix
        x = x + 0.5 * swiglu(rmsnorm(x, norm_g[layer, 2]), ffn_w_gate[layer, 1], ffn_w_up[layer, 1], ffn_w_down[layer, 1])
    return x


def setup_inputs(seed: int = 0) -> dict:
    key = jax.random.key(seed)
    ks = jax.random.split(key, 32)

    def nrm(k, shape, s):
        return jax.random.normal(k, shape, jnp.float32) * s

    max_decay = math.log(HYENA_DECAY_TARGET) / HYENA_FAST_DECAY_PCT
    min_decay = math.log(HYENA_DECAY_TARGET) / HYENA_SLOW_DECAY_PCT
    base_deltas = jnp.tile(jnp.linspace(min_decay, max_decay, HYENA_WIDTH, dtype=jnp.float32), 2)
    return {
        "x_prompt": nrm(ks[0], (BATCH, SEQ, D_MODEL), 1.0),
        "x_sample": nrm(ks[1], (DEC_BATCH, DEC_SEQ, D_MODEL), 1.0),
        "norm_g": 1.0 + nrm(ks[2], (DEPTH, 3, D_MODEL), 0.02),
        "ffn_w_gate": nrm(ks[3], (DEPTH, 2, D_MODEL, D_FF), D_MODEL ** -0.5),
        "ffn_w_up": nrm(ks[4], (DEPTH, 2, D_MODEL, D_FF), D_MODEL ** -0.5),
        "ffn_w_down": nrm(ks[5], (DEPTH, 2, D_FF, D_MODEL), D_FF ** -0.5),
        "w_in_ab": nrm(ks[6], (N_EVEN, D_MODEL, AB_IN), D_MODEL ** -0.5),
        "pool_w": nrm(ks[7], (N_EVEN, POOL_GROUPS, POOL_GROUP_W, POOL_GROUP_W), POOL_GROUP_W ** -0.5),
        "pool_scale": 1.0 + nrm(ks[8], (N_EVEN, POOL_WIDTH), 0.02),
        "short_w": nrm(ks[9], (N_EVEN, HYENA_SHORT, 3 * HYENA_WIDTH), HYENA_SHORT ** -0.5),
        "short_b": nrm(ks[10], (N_EVEN, 3 * HYENA_WIDTH), 0.01),
        "filt_w1": nrm(ks[11], (N_EVEN, HYENA_EMB, HYENA_FILTER_HIDDEN), HYENA_EMB ** -0.5),
        "filt_b1": nrm(ks[12], (N_EVEN, HYENA_FILTER_HIDDEN), 0.1),
        "filt_w2": nrm(ks[13], (N_EVEN, HYENA_FILTER_HIDDEN, HYENA_FILTER_HIDDEN), HYENA_FILTER_HIDDEN ** -0.5),
        "filt_b2": nrm(ks[14], (N_EVEN, HYENA_FILTER_HIDDEN), 0.1),
        "filt_w3": nrm(ks[15], (N_EVEN, HYENA_FILTER_HIDDEN, HYENA_FILTER_HIDDEN), HYENA_FILTER_HIDDEN ** -0.5),
        "filt_b3": nrm(ks[16], (N_EVEN, HYENA_FILTER_HIDDEN), 0.1),
        "filt_freq": 1.0 + nrm(ks[17], (N_EVEN, HYENA_FILTER_HIDDEN), 0.1),
        "filt_w_out": nrm(ks[18], (N_EVEN, HYENA_FILTER_HIDDEN, 2 * HYENA_WIDTH), 0.05 * HYENA_FILTER_HIDDEN ** -0.5),
        "filt_deltas": base_deltas[None, :] + nrm(ks[19], (N_EVEN, 2 * HYENA_WIDTH), 0.1),
        "hyena_bias": nrm(ks[20], (N_EVEN, HYENA_WIDTH), 0.1),
        "w_out_ab": nrm(ks[21], (N_EVEN, AB_OUT, D_MODEL), AB_OUT ** -0.5),
        "w_qkv": nrm(ks[22], (N_ODD, D_MODEL, 3 * D_MODEL), D_MODEL ** -0.5),
        "q_norm_g": 1.0 + nrm(ks[23], (N_ODD, HEAD_DIM), 0.02),
        "k_norm_g": 1.0 + nrm(ks[24], (N_ODD, HEAD_DIM), 0.02),
        "rpb": nrm(ks[25], (N_ODD, N_HEADS, 2 * NA_MAX_ROWS - 1, 2 * NA_COLS - 1), 0.1),
        "w_o": nrm(ks[26], (N_ODD, D_MODEL, D_MODEL), D_MODEL ** -0.5),
    }


def reference(x_prompt, x_sample, norm_g, ffn_w_gate, ffn_w_up, ffn_w_down, w_in_ab, pool_w, pool_scale,
              short_w, short_b, filt_w1, filt_b1, filt_w2, filt_b2, filt_w3, filt_b3, filt_freq, filt_w_out,
              filt_deltas, hyena_bias, w_out_ab, w_qkv, q_norm_g, k_norm_g, rpb, w_o):
    y_prompt = run_trunk(x_prompt, norm_g, ffn_w_gate, ffn_w_up, ffn_w_down, w_in_ab, pool_w, pool_scale,
                         short_w, short_b, filt_w1, filt_b1, filt_w2, filt_b2, filt_w3, filt_b3, filt_freq,
                         filt_w_out, filt_deltas, hyena_bias, w_out_ab, w_qkv, q_norm_g, k_norm_g, rpb, w_o)
    y_sample = run_trunk(x_sample, norm_g, ffn_w_gate, ffn_w_up, ffn_w_down, w_in_ab, pool_w, pool_scale,
                         short_w, short_b, filt_w1, filt_b1, filt_w2, filt_b2, filt_w3, filt_b3, filt_freq,
                         filt_w_out, filt_deltas, hyena_bias, w_out_ab, w_qkv, q_norm_g, k_norm_g, rpb, w_o)
    return (y_prompt, y_sample)
```

```python
import functools
import math

import jax
import jax.numpy as jnp
import numpy as np
from jax import lax
from jax.experimental import pallas as pl
from jax.experimental.pallas import tpu as pltpu

BF16 = jnp.bfloat16
F32 = jnp.float32

D_MODEL = 1024
SEQ = 2048
D_FF = 2816
EPS = 1e-6
GRID_W = 64
POOL_WIDTH = 512
POOL_WINDOWS = (2, 4, 8, 16)
HYENA_WIDTH = 512
HYENA_BANDS = 16
HYENA_EMB = 1 + 2 * HYENA_BANDS
HYENA_HIDDEN = 64
HYENA_MOD_SHIFT = 0.05
N_HEADS = 16
HEAD_DIM = 64
NA_ROWS = 8
NA_COLS = 16
ROWS = SEQ // GRID_W

LANES = 128
MXU_N = 256
VMEM_LIMIT_BYTES = 56 * 1024 * 1024

FFN_TOKENS = 512
FFN_CHUNK = MXU_N

DFT_N = 2 * SEQ
DFT_H = SEQ // 2
DFT_FP = DFT_H + 8
POOL_PAD = 8

ATT_QROWS = 2
ATT_KROWS = 10
ATT_Q = ATT_QROWS * GRID_W
ATT_K = ATT_KROWS * GRID_W
ATT_GROUPS = ROWS // ATT_QROWS
ATT_TYPES = 5
NEG_BIG = -1e30


def _params(n_axes=1):
    return pltpu.CompilerParams(dimension_semantics=("parallel",) * n_axes,
                                vmem_limit_bytes=VMEM_LIMIT_BYTES)


def _resident(shape, n_axes=1):
    zeros = (0,) * len(shape)
    if n_axes == 1:
        return pl.BlockSpec(shape, lambda i: zeros, pipeline_mode=pl.Buffered(1))
    return pl.BlockSpec(shape, lambda i, j: zeros, pipeline_mode=pl.Buffered(1))


def _dot(a, b):
    return jnp.dot(a, b, preferred_element_type=F32)


def _rms_scale(x, g):
    ms = jnp.mean(x * x, axis=-1, keepdims=True)
    return x * lax.rsqrt(ms + EPS) * g


def _ffn_kernel(n_pre, x_ref, *refs):
    pre = refs[:2 * n_pre]
    g_ref, wg_ref, wu_ref, wd_ref, o_ref = refs[2 * n_pre:]
    x = x_ref[...]
    for i in range(n_pre):
        x = x + _dot(pre[2 * i][...], pre[2 * i + 1][...])
    xn = _rms_scale(x, g_ref[...]).astype(BF16)
    acc = jnp.zeros(x.shape, F32)
    for c in range(D_FF // FFN_CHUNK):
        cols = slice(c * FFN_CHUNK, (c + 1) * FFN_CHUNK)
        gate = _dot(xn, wg_ref[:, cols])
        up = _dot(xn, wu_ref[:, cols])
        h = (gate / (1.0 + jnp.exp(-gate)) * up).astype(BF16)
        acc = acc + _dot(h, wd_ref[cols, :])
    o_ref[...] = x + 0.5 * acc


def ffn(x, g, wg, wu, wd, pre=()):
    n = x.shape[0]
    tile = pl.BlockSpec((FFN_TOKENS, D_MODEL), lambda i: (i, 0))
    in_specs, args = [tile], [x]
    for act, w in pre:
        in_specs += [pl.BlockSpec((FFN_TOKENS, act.shape[1]), lambda i: (i, 0)), _resident(w.shape)]
        args += [act, w]
    in_specs += [_resident((1, D_MODEL)), _resident((D_MODEL, D_FF)), _resident((D_MODEL, D_FF)),
                 _resident((D_FF, D_MODEL))]
    args += [g, wg, wu, wd]
    return pl.pallas_call(
        functools.partial(_ffn_kernel, len(pre)),
        out_shape=jax.ShapeDtypeStruct(x.shape, F32),
        grid=(n // FFN_TOKENS,),
        in_specs=in_specs,
        out_specs=tile,
        compiler_params=_params(),
        name="ffn",
    )(*args)


def _shift_rows(x, k):
    return pltpu.roll(x, k % x.shape[0], 0)


def _window_sum(ue, w):
    p = ue + _shift_rows(ue, 1)
    if w == 2:
        return p
    span = 2
    while 2 * span < w:
        p = p + _shift_rows(p, span)
        span *= 2
    return _shift_rows(p, 1) + _shift_rows(p, -(span - 1))


def _mix0_in_kernel(x_ref, g_ref, win_ref, pw_ref, ps_ref, sw_ref, sb_ref, a_ref, z_ref, x0_ref, pad_ref):
    xn = _rms_scale(x_ref[0], g_ref[...]).astype(BF16)
    row = lax.broadcasted_iota(jnp.int32, (SEQ, 1), 0)

    zeros = jnp.zeros((POOL_PAD, MXU_N), F32)
    pad_ref[0:POOL_PAD, :] = zeros
    pad_ref[POOL_PAD + SEQ:POOL_PAD + SEQ + POOL_PAD, :] = zeros
    for pp in range(POOL_WIDTH // MXU_N):
        u = _dot(xn, win_ref[:, pp * MXU_N:(pp + 1) * MXU_N])
        pad_ref[POOL_PAD:POOL_PAD + SEQ, :] = u
        ue = pad_ref[...]
        halves = []
        for j in range(MXU_N // LANES):
            w = POOL_WINDOWS[pp * (MXU_N // LANES) + j]
            lanes = slice(j * LANES, (j + 1) * LANES)
            win = _window_sum(ue[:, lanes], w)[POOL_PAD:POOL_PAD + SEQ]
            lo = jnp.maximum(row - w // 2, 0)
            hi = jnp.minimum(row + (w - w // 2 - 1), SEQ - 1)
            cnt = (hi - lo + 1).astype(F32)
            halves.append(win / cnt - u[:, lanes])
        pooled = jnp.concatenate(halves, axis=1).astype(BF16)
        mixed = _dot(pooled, pw_ref[pp]) * ps_ref[:, pp * MXU_N:(pp + 1) * MXU_N]
        a_ref[0, :, pp * MXU_N:(pp + 1) * MXU_N] = mixed.astype(BF16)

    has_prev = row >= 1
    has_next = row <= SEQ - 2
    for cc in range(HYENA_WIDTH // MXU_N):
        parts = []
        for part in range(3):
            c0 = part * HYENA_WIDTH + cc * MXU_N
            p = _dot(xn, win_ref[:, POOL_WIDTH + c0:POOL_WIDTH + c0 + MXU_N])
            w = sw_ref[:, c0:c0 + MXU_N]
            prev = jnp.where(has_prev, _shift_rows(p, 1), 0.0)
            nxt = jnp.where(has_next, _shift_rows(p, -1), 0.0)
            parts.append(prev * w[0:1] + p * w[1:2] + nxt * w[2:3] + sb_ref[:, c0:c0 + MXU_N])
        x0c, x1c, vc = parts
        zc = vc * x1c
        for j in range(MXU_N // LANES):
            z_ref[0, cc * (MXU_N // LANES) + j] = zc[:, j * LANES:(j + 1) * LANES]
        x0_ref[0, :, cc * MXU_N:(cc + 1) * MXU_N] = x0c


def mix0_in(x, g, w_in, pool_w2, pool_scale, short_w, short_b):
    b = x.shape[0]
    return pl.pallas_call(
        _mix0_in_kernel,
        out_shape=(jax.ShapeDtypeStruct((b, SEQ, POOL_WIDTH), BF16),
                   jax.ShapeDtypeStruct((b, HYENA_WIDTH // LANES, SEQ, LANES), F32),
                   jax.ShapeDtypeStruct((b, SEQ, HYENA_WIDTH), F32)),
        grid=(b,),
        in_specs=[
            pl.BlockSpec((1, SEQ, D_MODEL), lambda i: (i, 0, 0)),
            _resident((1, D_MODEL)),
            _resident(w_in.shape),
            _resident(pool_w2.shape),
            _resident((1, POOL_WIDTH)),
            _resident(short_w.shape),
            _resident(short_b.shape),
        ],
        out_specs=(pl.BlockSpec((1, SEQ, POOL_WIDTH), lambda i: (i, 0, 0)),
                   pl.BlockSpec((1, HYENA_WIDTH // LANES, SEQ, LANES), lambda i: (i, 0, 0, 0)),
                   pl.BlockSpec((1, SEQ, HYENA_WIDTH), lambda i: (i, 0, 0))),
        scratch_shapes=[pltpu.VMEM((SEQ + 2 * POOL_PAD, MXU_N), F32)],
        compiler_params=_params(),
        name="mix0_in",
    )(x, g, w_in, pool_w2, pool_scale, short_w, short_b)


def _dft_matrices():
    f = np.arange(DFT_FP, dtype=np.int64)[:, None]
    s = np.arange(DFT_H, dtype=np.int64)[None, :]
    keep = f <= DFT_H
    unit = 2.0 * np.pi / DFT_N
    ang_e = unit * ((f * (2 * s)) % DFT_N)
    ang_o = unit * ((f * (2 * s + 1)) % DFT_N)
    fwd = [np.where(keep, m, 0.0) for m in (np.cos(ang_e), np.sin(ang_e), np.cos(ang_o), np.sin(ang_o))]
    t = np.arange(DFT_H, dtype=np.int64)[:, None]
    fi = np.arange(DFT_H, dtype=np.int64)[None, :]
    wgt = np.where(fi == 0, 1.0, 2.0) / DFT_N
    ph_e = unit * ((fi * (2 * t)) % DFT_N)
    ph_o = unit * ((fi * (2 * t + 1)) % DFT_N)
    inv = [wgt * np.cos(ph_e), -wgt * np.sin(ph_e), wgt * np.cos(ph_o), -wgt * np.sin(ph_o)]
    return fwd, inv


def _filter_embedding():
    t_idx = np.arange(SEQ, dtype=np.float64)
    t_norm = np.linspace(0.0, 1.0, SEQ, dtype=np.float64)
    bands = np.linspace(1e-4, HYENA_BANDS - 1, HYENA_BANDS, dtype=np.float64)
    ang = (2.0 * math.pi / SEQ) * t_idx[:, None] * bands[None, :]
    out = np.zeros((SEQ, LANES), np.float64)
    out[:, :HYENA_EMB] = np.concatenate([t_norm[:, None], np.cos(ang), -np.sin(ang)], axis=-1)
    return out, t_norm[:, None]


def _strided_rows(ref, chunk, parity):
    return ref[chunk, pl.ds(parity, DFT_H, stride=2), :]


def _filter_kernel(emb_ref, tn_ref, w1_ref, b1_ref, w2_ref, b2_ref, w3_ref, b3_ref, fr_ref, wo_ref, dl_ref,
                   ce_ref, se_ref, co_ref, so_ref, kpr_ref, kpi_ref, kmr_ref, kmi_ref, s_ref, d_ref):
    fr = fr_ref[...]
    h = jnp.sin(fr * (_dot(emb_ref[...].astype(BF16), w1_ref[...]) + b1_ref[...]))
    h = jnp.sin(fr * (_dot(h.astype(BF16), w2_ref[...]) + b2_ref[...]))
    h = jnp.sin(fr * (_dot(h.astype(BF16), w3_ref[...]) + b3_ref[...]))
    ho = _dot(h.astype(BF16), wo_ref[...])
    decay = jnp.exp(-tn_ref[...] * jnp.abs(dl_ref[...]))
    hm = ho * (decay + HYENA_MOD_SHIFT)
    row = lax.broadcasted_iota(jnp.int32, (SEQ, 1), 0)
    fwd = hm[:, :HYENA_WIDTH]
    bwd = jnp.where(row == 0, 0.0, hm[:, HYENA_WIDTH:])
    ssum = fwd + bwd
    sdif = bwd - fwd
    n_chunks = HYENA_WIDTH // LANES
    for c in range(n_chunks):
        s_ref[c] = ssum[:, c * LANES:(c + 1) * LANES]
        d_ref[c] = sdif[:, c * LANES:(c + 1) * LANES]

    def parity(ref, par):
        return jnp.concatenate([_strided_rows(ref, c, par) for c in range(n_chunks)], axis=1).astype(BF16)

    kpr_ref[...] = 2.0 * _dot(ce_ref[...], parity(s_ref, 0))
    kpi_ref[...] = 2.0 * _dot(se_ref[...], parity(d_ref, 0))
    kmr_ref[...] = 2.0 * _dot(co_ref[...], parity(s_ref, 1))
    kmi_ref[...] = 2.0 * _dot(so_ref[...], parity(d_ref, 1))


def hyena_filter(emb, tnorm, w1, b1, w2, b2, w3, b3, freq, w_out, deltas, fwd_mats):
    spec = jax.ShapeDtypeStruct((DFT_FP, HYENA_WIDTH), F32)
    return pl.pallas_call(
        _filter_kernel,
        out_shape=(spec,) * 4,
        scratch_shapes=[pltpu.VMEM((HYENA_WIDTH // LANES, SEQ, LANES), F32)] * 2,
        compiler_params=pltpu.CompilerParams(vmem_limit_bytes=VMEM_LIMIT_BYTES),
        name="hyena_filter",
    )(emb, tnorm, w1, b1, w2, b2, w3, b3, freq, w_out, deltas, *fwd_mats)


def _hyena_conv_kernel(z_ref, x0_ref, hb_ref, kpr_ref, kpi_ref, kmr_ref, kmi_ref,
                       ce_ref, se_ref, co_ref, so_ref, ice_ref, ise_ref, ico_ref, iso_ref, o_ref, y_ref):
    n_chunks = MXU_N // LANES
    ze = jnp.concatenate([_strided_rows(z_ref.at[0], c, 0) for c in range(n_chunks)], axis=1).astype(BF16)
    zo = jnp.concatenate([_strided_rows(z_ref.at[0], c, 1) for c in range(n_chunks)], axis=1).astype(BF16)
    er = _dot(ce_ref[...], ze)
    ei = _dot(se_ref[...], ze)
    orr = _dot(co_ref[...], zo)
    oi = _dot(so_ref[...], zo)
    kpr, kpi, kmr, kmi = kpr_ref[...], kpi_ref[...], kmr_ref[...], kmi_ref[...]
    a_r = er * kpr + ei * kpi + orr * kmr + oi * kmi
    a_i = er * kpi - ei * kpr + orr * kmi - oi * kmr
    d_r = er * kmr + ei * kmi + orr * kpr + oi * kpi
    d_i = er * kmi - ei * kmr + orr * kpi - oi * kpr
    trow = lax.broadcasted_iota(jnp.int32, (DFT_H, 1), 0)
    sign = jnp.where((trow & 1) == 0, 1.0 / DFT_N, -1.0 / DFT_N)
    ye = (_dot(ice_ref[...], a_r[:DFT_H].astype(BF16)) + _dot(ise_ref[...], a_i[:DFT_H].astype(BF16))
          + sign * a_r[DFT_H:DFT_H + 1])
    yo = (_dot(ico_ref[...], d_r[:DFT_H].astype(BF16)) + _dot(iso_ref[...], d_i[:DFT_H].astype(BF16))
          - sign * d_i[DFT_H:DFT_H + 1])
    for c in range(n_chunks):
        lanes = slice(c * LANES, (c + 1) * LANES)
        y_ref[c, pl.ds(0, DFT_H, stride=2), :] = ye[:, lanes]
        y_ref[c, pl.ds(1, DFT_H, stride=2), :] = yo[:, lanes]
    for c in range(n_chunks):
        lanes = slice(c * LANES, (c + 1) * LANES)
        y = y_ref[c] + z_ref[0, c] * hb_ref[:, lanes]
        o_ref[0, :, lanes] = (y * x0_ref[0, :, lanes]).astype(BF16)


def hyena_conv(z, x0, h_bias, spectra, fwd_mats, inv_mats):
    b = z.shape[0]
    n_cc = HYENA_WIDTH // MXU_N
    spec_block = pl.BlockSpec((DFT_FP, MXU_N), lambda c, i: (0, c))
    return pl.pallas_call(
        _hyena_conv_kernel,
        out_shape=jax.ShapeDtypeStruct((b, SEQ, HYENA_WIDTH), BF16),
        grid=(n_cc, b),
        in_specs=[
            pl.BlockSpec((1, MXU_N // LANES, SEQ, LANES), lambda c, i: (i, c, 0, 0)),
            pl.BlockSpec((1, SEQ, MXU_N), lambda c, i: (i, 0, c)),
            pl.BlockSpec((1, MXU_N), lambda c, i: (0, c)),
            spec_block, spec_block, spec_block, spec_block,
        ] + [_resident(m.shape, 2) for m in fwd_mats + inv_mats],
        out_specs=pl.BlockSpec((1, SEQ, MXU_N), lambda c, i: (i, 0, c)),
        scratch_shapes=[pltpu.VMEM((MXU_N // LANES, SEQ, LANES), F32)],
        compiler_params=_params(2),
        name="hyena_conv",
    )(z, x0, h_bias, *spectra, *fwd_mats, *inv_mats)


def _head_rms(t, g):
    low = lax.broadcasted_iota(jnp.int32, (1, LANES), 1) < HEAD_DIM
    blocks = []
    for j in range(D_MODEL // LANES):
        blk = t[:, j * LANES:(j + 1) * LANES]
        sq = blk * blk
        s_lo = jnp.sum(jnp.where(low, sq, 0.0), axis=-1, keepdims=True)
        s_hi = jnp.sum(jnp.where(low, 0.0, sq), axis=-1, keepdims=True)
        ms = jnp.where(low, s_lo, s_hi) * (1.0 / HEAD_DIM)
        blocks.append(blk * lax.rsqrt(ms + EPS))
    return jnp.concatenate(blocks, axis=1) * g


def _qkv_kernel(x_ref, g_ref, w_ref, gq_ref, gk_ref, q_ref, k_ref, v_ref):
    xn = _rms_scale(x_ref[...], g_ref[...]).astype(BF16)
    q = _dot(xn, w_ref[:, 0:D_MODEL])
    q_ref[...] = (_head_rms(q, gq_ref[...]) * (HEAD_DIM ** -0.5)).astype(BF16)
    k = _dot(xn, w_ref[:, D_MODEL:2 * D_MODEL])
    k_ref[...] = _head_rms(k, gk_ref[...]).astype(BF16)
    v_ref[...] = _dot(xn, w_ref[:, 2 * D_MODEL:3 * D_MODEL]).astype(BF16)


def qkv_proj(x, g, w_qkv, gq, gk):
    n = x.shape[0]
    tile = pl.BlockSpec((FFN_TOKENS, D_MODEL), lambda i: (i, 0))
    out = jax.ShapeDtypeStruct((n, D_MODEL), BF16)
    return pl.pallas_call(
        _qkv_kernel,
        out_shape=(out, out, out),
        grid=(n // FFN_TOKENS,),
        in_specs=[tile, _resident((1, D_MODEL)), _resident(w_qkv.shape), _resident((1, D_MODEL)),
                  _resident((1, D_MODEL))],
        out_specs=(tile, tile, tile),
        compiler_params=_params(),
        name="qkv_proj",
    )(x, g, w_qkv, gq, gk)


def _att_group_start(gi):
    return min(max(ATT_QROWS * gi - NA_ROWS // 2, 0), ROWS - ATT_KROWS)


def _att_group_type(gi):
    if gi < 2:
        return gi
    if gi >= ATT_GROUPS - 2:
        return ATT_TYPES - (ATT_GROUPS - gi)
    return 2


def _attn_kernel(q_ref, k_ref, v_ref, bias_ref, o_ref):
    kt = k_ref[0].T
    low = lax.broadcasted_iota(jnp.int32, (1, LANES), 1) < HEAD_DIM
    for gi in range(ATT_GROUPS):
        q0 = gi * ATT_Q
        k0 = _att_group_start(gi) * GRID_W
        ty = _att_group_type(gi)
        qg = q_ref[0, q0:q0 + ATT_Q, :]
        kt_blk = kt[:, k0:k0 + ATT_K]
        v_blk = v_ref[0, k0:k0 + ATT_K, :]
        outs = []
        for hh in range(2):
            keep = low if hh == 0 else jnp.logical_not(low)
            qm = jnp.where(keep, qg, jnp.zeros_like(qg))
            s = _dot(qm, kt_blk) + bias_ref[hh, ty]
            m = jnp.max(s, axis=-1, keepdims=True)
            p = jnp.exp(s - m)
            den = jnp.sum(p, axis=-1, keepdims=True)
            outs.append(_dot(p.astype(BF16), v_blk) / den)
        o_ref[0, q0:q0 + ATT_Q, :] = jnp.where(low, outs[0], outs[1]).astype(BF16)


def attention(q, k, v, bias):
    b = q.shape[0]
    blk = pl.BlockSpec((1, SEQ, LANES), lambda h, i: (i, 0, h))
    return pl.pallas_call(
        _attn_kernel,
        out_shape=jax.ShapeDtypeStruct((b, SEQ, D_MODEL), BF16),
        grid=(N_HEADS // 2, b),
        in_specs=[blk, blk, blk, pl.BlockSpec((2, ATT_TYPES, ATT_Q, ATT_K), lambda h, i: (h, 0, 0, 0))],
        out_specs=blk,
        compiler_params=_params(2),
        name="attention",
    )(q, k, v, bias)


def _attn_bias_panels(rpb):
    c = np.arange(GRID_W)
    c_start = np.clip(c - NA_COLS // 2, 0, GRID_W - NA_COLS)
    col_ok = (c[None, :] >= c_start[:, None]) & (c[None, :] < c_start[:, None] + NA_COLS)
    dc = np.clip(c[None, :] - c[:, None] + NA_COLS - 1, 0, 2 * NA_COLS - 2)
    tab = jnp.where(col_ok[None, None], rpb[:, :, dc].astype(F32), NEG_BIG)
    tab = jnp.concatenate([tab, jnp.full((N_HEADS, 1, GRID_W, GRID_W), NEG_BIG, F32)], axis=1)
    invalid = 2 * NA_ROWS - 1
    dr = np.full((ATT_TYPES, ATT_QROWS, ATT_KROWS), invalid, np.int32)
    reps = {0: 0, 1: 1, 2: 2, 3: ATT_GROUPS - 2, 4: ATT_GROUPS - 1}
    for ty, gi in reps.items():
        start = _att_group_start(gi)
        for j in range(ATT_QROWS):
            qrow = ATT_QROWS * gi + j
            s_q = min(max(qrow - NA_ROWS // 2, 0), ROWS - NA_ROWS)
            for i in range(ATT_KROWS):
                krow = start + i
                if s_q <= krow < s_q + NA_ROWS:
                    dr[ty, j, i] = krow - qrow + NA_ROWS - 1
    panels = tab[:, dr.reshape(-1)]
    panels = panels.reshape(N_HEADS, ATT_TYPES, ATT_QROWS, ATT_KROWS, GRID_W, GRID_W)
    panels = panels.transpose(0, 1, 2, 4, 3, 5)
    return panels.reshape(N_HEADS, ATT_TYPES, ATT_Q, ATT_K)


def _block_diag_pairs(pool_w):
    z = jnp.zeros((LANES, LANES), pool_w.dtype)
    pairs = []
    for pp in range(2):
        top = jnp.concatenate([pool_w[2 * pp], z], axis=1)
        bot = jnp.concatenate([z, pool_w[2 * pp + 1]], axis=1)
        pairs.append(jnp.concatenate([top, bot], axis=0))
    return jnp.stack(pairs)


def kernel(x_prompt, x_sample, norm_g, ffn_w_gate, ffn_w_up, ffn_w_down, w_in_ab, pool_w, pool_scale, short_w, short_b, filt_w1, filt_b1, filt_w2, filt_b2, filt_w3, filt_b3, filt_freq, filt_w_out, filt_deltas, hyena_bias, w_out_ab, w_qkv, q_norm_g, k_norm_g, rpb, w_o):
    wg = ffn_w_gate.astype(BF16)
    wu = ffn_w_up.astype(BF16)
    wd = ffn_w_down.astype(BF16)
    w_in = w_in_ab[0].astype(BF16)
    w_out = w_out_ab[0].astype(BF16)
    pool_w2 = _block_diag_pairs(pool_w[0]).astype(BF16)
    w_qkv_b = w_qkv[0].astype(BF16)
    w_o_b = w_o[0].astype(BF16)
    gq = jnp.tile(q_norm_g[0], N_HEADS)[None]
    gk = jnp.tile(k_norm_g[0], N_HEADS)[None]

    fwd_np, inv_np = _dft_matrices()
    fwd_mats = [jnp.asarray(m).astype(BF16) for m in fwd_np]
    inv_mats = [jnp.asarray(m).astype(BF16) for m in inv_np]
    emb_np, tnorm_np = _filter_embedding()
    w1 = jnp.zeros((LANES, HYENA_HIDDEN), BF16).at[:HYENA_EMB].set(filt_w1[0].astype(BF16))
    spectra = hyena_filter(jnp.asarray(emb_np).astype(F32), jnp.asarray(tnorm_np).astype(F32), w1, filt_b1[0][None],
                           filt_w2[0].astype(BF16), filt_b2[0][None], filt_w3[0].astype(BF16), filt_b3[0][None],
                           filt_freq[0][None], filt_w_out[0].astype(BF16), filt_deltas[0][None], fwd_mats)
    bias = _attn_bias_panels(rpb[0])

    outs = []
    for x in (x_prompt, x_sample):
        b = x.shape[0]
        n = b * SEQ
        t = ffn(x.reshape(n, D_MODEL), norm_g[0, 0][None], wg[0, 0], wu[0, 0], wd[0, 0])
        a, z, x0 = mix0_in(t.reshape(b, SEQ, D_MODEL), norm_g[0, 1][None], w_in, pool_w2, pool_scale[0][None],
                           short_w[0], short_b[0][None])
        yh = hyena_conv(z, x0, hyena_bias[0][None], spectra, fwd_mats, inv_mats)
        t = ffn(t, norm_g[0, 2][None], wg[0, 1], wu[0, 1], wd[0, 1],
                pre=((a.reshape(n, POOL_WIDTH), w_out[:POOL_WIDTH]), (yh.reshape(n, HYENA_WIDTH), w_out[POOL_WIDTH:])))
        t = ffn(t, norm_g[1, 0][None], wg[1, 0], wu[1, 0], wd[1, 0])
        q, k, v = qkv_proj(t, norm_g[1, 1][None], w_qkv_b, gq, gk)
        o = attention(q.reshape(b, SEQ, D_MODEL), k.reshape(b, SEQ, D_MODEL), v.reshape(b, SEQ, D_MODEL), bias)
        t = ffn(t, norm_g[1, 2][None], wg[1, 1], wu[1, 1], wd[1, 1], pre=((o.reshape(n, D_MODEL), w_o_b),))
        outs.append(t.reshape(b, SEQ, D_MODEL))
    return tuple(outs)
```

```python
import functools
import math

import jax
import jax.numpy as jnp
import numpy as np
from jax import lax
from jax.experimental import pallas as pl
from jax.experimental.pallas import tpu as pltpu

BF16 = jnp.bfloat16
F32 = jnp.float32

D_MODEL = 1024
SEQ = 2048
D_FF = 2816
EPS = 1e-6
GRID_W = 64
POOL_WIDTH = 512
POOL_WINDOWS = (2, 4, 8, 16)
HYENA_WIDTH = 512
HYENA_BANDS = 16
HYENA_EMB = 1 + 2 * HYENA_BANDS
HYENA_HIDDEN = 64
HYENA_MOD_SHIFT = 0.05
N_HEADS = 16
HEAD_DIM = 64
NA_ROWS = 8
NA_COLS = 16
ROWS = SEQ // GRID_W

LANES = 128
MXU_N = 256
VMEM_LIMIT_BYTES = 56 * 1024 * 1024

FFN_TOKENS = 512
FFN_CHUNK = MXU_N

DFT_N = 2 * SEQ
DFT_H = SEQ // 2
DFT_FP = DFT_H + 8
POOL_PAD = 8

ATT_QROWS = 2
ATT_KROWS = 10
ATT_Q = ATT_QROWS * GRID_W
ATT_K = ATT_KROWS * GRID_W
ATT_GROUPS = ROWS // ATT_QROWS
ATT_TYPES = 5
ATT_DR_INVALID = 2 * NA_ROWS - 1
ATT_ONES = 16
NEG_BIG = -1e30


def _params(n_axes=1):
    return pltpu.CompilerParams(dimension_semantics=("parallel",) * n_axes,
                                vmem_limit_bytes=VMEM_LIMIT_BYTES)


def _resident(shape, n_axes=1):
    zeros = (0,) * len(shape)
    if n_axes == 1:
        return pl.BlockSpec(shape, lambda i: zeros, pipeline_mode=pl.Buffered(1))
    return pl.BlockSpec(shape, lambda i, j: zeros, pipeline_mode=pl.Buffered(1))


def _dot(a, b):
    return jnp.dot(a, b, preferred_element_type=F32)


def _rms_scale(x, g):
    ms = jnp.mean(x * x, axis=-1, keepdims=True)
    return x * lax.rsqrt(ms + EPS) * g


def _ffn_kernel(n_pre, x_ref, *refs):
    pre = refs[:2 * n_pre]
    g_ref, wg_ref, wu_ref, wd_ref, o_ref = refs[2 * n_pre:]
    x = x_ref[...]
    for i in range(n_pre):
        x = x + _dot(pre[2 * i][...], pre[2 * i + 1][...])
    xn = _rms_scale(x, g_ref[...]).astype(BF16)
    acc = jnp.zeros(x.shape, F32)
    for c in range(D_FF // FFN_CHUNK):
        cols = slice(c * FFN_CHUNK, (c + 1) * FFN_CHUNK)
        gate = _dot(xn, wg_ref[:, cols])
        up = _dot(xn, wu_ref[:, cols])
        h = (gate / (1.0 + jnp.exp(-gate)) * up).astype(BF16)
        acc = acc + _dot(h, wd_ref[cols, :])
    o_ref[...] = x + 0.5 * acc


def ffn(x, g, wg, wu, wd, pre=()):
    n = x.shape[0]
    tile = pl.BlockSpec((FFN_TOKENS, D_MODEL), lambda i: (i, 0))
    in_specs, args = [tile], [x]
    for act, w in pre:
        in_specs += [pl.BlockSpec((FFN_TOKENS, act.shape[1]), lambda i: (i, 0)), _resident(w.shape)]
        args += [act, w]
    in_specs += [_resident((1, D_MODEL)), _resident((D_MODEL, D_FF)), _resident((D_MODEL, D_FF)),
                 _resident((D_FF, D_MODEL))]
    args += [g, wg, wu, wd]
    return pl.pallas_call(
        functools.partial(_ffn_kernel, len(pre)),
        out_shape=jax.ShapeDtypeStruct(x.shape, F32),
        grid=(n // FFN_TOKENS,),
        in_specs=in_specs,
        out_specs=tile,
        compiler_params=_params(),
        name="ffn",
    )(*args)


def _shift_rows(x, k):
    return pltpu.roll(x, k % x.shape[0], 0)


def _window_sum(ue, w):
    p = ue + _shift_rows(ue, 1)
    if w == 2:
        return p
    span = 2
    while 2 * span < w:
        p = p + _shift_rows(p, span)
        span *= 2
    return _shift_rows(p, 1) + _shift_rows(p, -(span - 1))


def _mix0_in_kernel(x_ref, g_ref, win_ref, pw_ref, ps_ref, sw_ref, sb_ref, a_ref, z_ref, x0_ref, pad_ref):
    xn = _rms_scale(x_ref[0], g_ref[...]).astype(BF16)
    row = lax.broadcasted_iota(jnp.int32, (SEQ, 1), 0)

    zeros = jnp.zeros((POOL_PAD, MXU_N), F32)
    pad_ref[0:POOL_PAD, :] = zeros
    pad_ref[POOL_PAD + SEQ:POOL_PAD + SEQ + POOL_PAD, :] = zeros
    for pp in range(POOL_WIDTH // MXU_N):
        u = _dot(xn, win_ref[:, pp * MXU_N:(pp + 1) * MXU_N])
        pad_ref[POOL_PAD:POOL_PAD + SEQ, :] = u
        ue = pad_ref[...]
        halves = []
        for j in range(MXU_N // LANES):
            w = POOL_WINDOWS[pp * (MXU_N // LANES) + j]
            lanes = slice(j * LANES, (j + 1) * LANES)
            win = _window_sum(ue[:, lanes], w)[POOL_PAD:POOL_PAD + SEQ]
            lo = jnp.maximum(row - w // 2, 0)
            hi = jnp.minimum(row + (w - w // 2 - 1), SEQ - 1)
            cnt = (hi - lo + 1).astype(F32)
            halves.append(win / cnt - u[:, lanes])
        pooled = jnp.concatenate(halves, axis=1).astype(BF16)
        mixed = _dot(pooled, pw_ref[pp]) * ps_ref[:, pp * MXU_N:(pp + 1) * MXU_N]
        a_ref[0, :, pp * MXU_N:(pp + 1) * MXU_N] = mixed.astype(BF16)

    has_prev = row >= 1
    has_next = row <= SEQ - 2
    for cc in range(HYENA_WIDTH // MXU_N):
        parts = []
        for part in range(3):
            c0 = part * HYENA_WIDTH + cc * MXU_N
            p = _dot(xn, win_ref[:, POOL_WIDTH + c0:POOL_WIDTH + c0 + MXU_N])
            w = sw_ref[:, c0:c0 + MXU_N]
            prev = jnp.where(has_prev, _shift_rows(p, 1), 0.0)
            nxt = jnp.where(has_next, _shift_rows(p, -1), 0.0)
            parts.append(prev * w[0:1] + p * w[1:2] + nxt * w[2:3] + sb_ref[:, c0:c0 + MXU_N])
        x0c, x1c, vc = parts
        zc = vc * x1c
        for j in range(MXU_N // LANES):
            z_ref[0, cc * (MXU_N // LANES) + j] = zc[:, j * LANES:(j + 1) * LANES]
        x0_ref[0, :, cc * MXU_N:(cc + 1) * MXU_N] = x0c


def mix0_in(x, g, w_in, pool_w2, pool_scale, short_w, short_b):
    b = x.shape[0]
    return pl.pallas_call(
        _mix0_in_kernel,
        out_shape=(jax.ShapeDtypeStruct((b, SEQ, POOL_WIDTH), BF16),
                   jax.ShapeDtypeStruct((b, HYENA_WIDTH // LANES, SEQ, LANES), F32),
                   jax.ShapeDtypeStruct((b, SEQ, HYENA_WIDTH), F32)),
        grid=(b,),
        in_specs=[
            pl.BlockSpec((1, SEQ, D_MODEL), lambda i: (i, 0, 0)),
            _resident((1, D_MODEL)),
            _resident(w_in.shape),
            _resident(pool_w2.shape),
            _resident((1, POOL_WIDTH)),
            _resident(short_w.shape),
            _resident(short_b.shape),
        ],
        out_specs=(pl.BlockSpec((1, SEQ, POOL_WIDTH), lambda i: (i, 0, 0)),
                   pl.BlockSpec((1, HYENA_WIDTH // LANES, SEQ, LANES), lambda i: (i, 0, 0, 0)),
                   pl.BlockSpec((1, SEQ, HYENA_WIDTH), lambda i: (i, 0, 0))),
        scratch_shapes=[pltpu.VMEM((SEQ + 2 * POOL_PAD, MXU_N), F32)],
        compiler_params=_params(),
        name="mix0_in",
    )(x, g, w_in, pool_w2, pool_scale, short_w, short_b)


def _dft_matrices():
    f = np.arange(DFT_FP, dtype=np.int64)[:, None]
    s = np.arange(DFT_H, dtype=np.int64)[None, :]
    keep = f <= DFT_H
    unit = 2.0 * np.pi / DFT_N
    ang_e = unit * ((f * (2 * s)) % DFT_N)
    ang_o = unit * ((f * (2 * s + 1)) % DFT_N)
    fwd = [np.where(keep, m, 0.0) for m in (np.cos(ang_e), np.sin(ang_e), np.cos(ang_o), np.sin(ang_o))]
    t = np.arange(DFT_H, dtype=np.int64)[:, None]
    fi = np.arange(DFT_H, dtype=np.int64)[None, :]
    wgt = np.where(fi == 0, 1.0, 2.0) / DFT_N
    ph_e = unit * ((fi * (2 * t)) % DFT_N)
    ph_o = unit * ((fi * (2 * t + 1)) % DFT_N)
    inv = [wgt * np.cos(ph_e), -wgt * np.sin(ph_e), wgt * np.cos(ph_o), -wgt * np.sin(ph_o)]
    return fwd, inv


def _filter_embedding():
    t_idx = np.arange(SEQ, dtype=np.float64)
    t_norm = np.linspace(0.0, 1.0, SEQ, dtype=np.float64)
    bands = np.linspace(1e-4, HYENA_BANDS - 1, HYENA_BANDS, dtype=np.float64)
    ang = (2.0 * math.pi / SEQ) * t_idx[:, None] * bands[None, :]
    out = np.zeros((SEQ, LANES), np.float64)
    out[:, :HYENA_EMB] = np.concatenate([t_norm[:, None], np.cos(ang), -np.sin(ang)], axis=-1)
    return out, t_norm[:, None]


def _strided_rows(ref, chunk, parity):
    return ref[chunk, pl.ds(parity, DFT_H, stride=2), :]


def _filter_kernel(emb_ref, tn_ref, w1_ref, b1_ref, w2_ref, b2_ref, w3_ref, b3_ref, fr_ref, wo_ref, dl_ref,
                   ce_ref, se_ref, co_ref, so_ref, kpr_ref, kpi_ref, kmr_ref, kmi_ref, s_ref, d_ref):
    fr = fr_ref[...]
    h = jnp.sin(fr * (_dot(emb_ref[...].astype(BF16), w1_ref[...]) + b1_ref[...]))
    h = jnp.sin(fr * (_dot(h.astype(BF16), w2_ref[...]) + b2_ref[...]))
    h = jnp.sin(fr * (_dot(h.astype(BF16), w3_ref[...]) + b3_ref[...]))
    ho = _dot(h.astype(BF16), wo_ref[...])
    decay = jnp.exp(-tn_ref[...] * jnp.abs(dl_ref[...]))
    hm = ho * (decay + HYENA_MOD_SHIFT)
    row = lax.broadcasted_iota(jnp.int32, (SEQ, 1), 0)
    fwd = hm[:, :HYENA_WIDTH]
    bwd = jnp.where(row == 0, 0.0, hm[:, HYENA_WIDTH:])
    ssum = fwd + bwd
    sdif = bwd - fwd
    n_chunks = HYENA_WIDTH // LANES
    for c in range(n_chunks):
        s_ref[c] = ssum[:, c * LANES:(c + 1) * LANES]
        d_ref[c] = sdif[:, c * LANES:(c + 1) * LANES]

    def parity(ref, par):
        return jnp.concatenate([_strided_rows(ref, c, par) for c in range(n_chunks)], axis=1).astype(BF16)

    kpr_ref[...] = 2.0 * _dot(ce_ref[...], parity(s_ref, 0))
    kpi_ref[...] = 2.0 * _dot(se_ref[...], parity(d_ref, 0))
    kmr_ref[...] = 2.0 * _dot(co_ref[...], parity(s_ref, 1))
    kmi_ref[...] = 2.0 * _dot(so_ref[...], parity(d_ref, 1))


def hyena_filter(emb, tnorm, w1, b1, w2, b2, w3, b3, freq, w_out, deltas, fwd_mats):
    spec = jax.ShapeDtypeStruct((DFT_FP, HYENA_WIDTH), F32)
    return pl.pallas_call(
        _filter_kernel,
        out_shape=(spec,) * 4,
        scratch_shapes=[pltpu.VMEM((HYENA_WIDTH // LANES, SEQ, LANES), F32)] * 2,
        compiler_params=pltpu.CompilerParams(vmem_limit_bytes=VMEM_LIMIT_BYTES),
        name="hyena_filter",
    )(emb, tnorm, w1, b1, w2, b2, w3, b3, freq, w_out, deltas, *fwd_mats)


def _hyena_conv_kernel(z_ref, x0_ref, hb_ref, kpr_ref, kpi_ref, kmr_ref, kmi_ref,
                       ce_ref, se_ref, co_ref, so_ref, ice_ref, ise_ref, ico_ref, iso_ref, o_ref, y_ref):
    n_chunks = MXU_N // LANES
    ze = jnp.concatenate([_strided_rows(z_ref.at[0], c, 0) for c in range(n_chunks)], axis=1).astype(BF16)
    zo = jnp.concatenate([_strided_rows(z_ref.at[0], c, 1) for c in range(n_chunks)], axis=1).astype(BF16)
    er = _dot(ce_ref[...], ze)
    ei = _dot(se_ref[...], ze)
    orr = _dot(co_ref[...], zo)
    oi = _dot(so_ref[...], zo)
    kpr, kpi, kmr, kmi = kpr_ref[...], kpi_ref[...], kmr_ref[...], kmi_ref[...]
    a_r = er * kpr + ei * kpi + orr * kmr + oi * kmi
    a_i = er * kpi - ei * kpr + orr * kmi - oi * kmr
    d_r = er * kmr + ei * kmi + orr * kpr + oi * kpi
    d_i = er * kmi - ei * kmr + orr * kpi - oi * kpr
    trow = lax.broadcasted_iota(jnp.int32, (DFT_H, 1), 0)
    sign = jnp.where((trow & 1) == 0, 1.0 / DFT_N, -1.0 / DFT_N)
    ye = (_dot(ice_ref[...], a_r[:DFT_H].astype(BF16)) + _dot(ise_ref[...], a_i[:DFT_H].astype(BF16))
          + sign * a_r[DFT_H:DFT_H + 1])
    yo = (_dot(ico_ref[...], d_r[:DFT_H].astype(BF16)) + _dot(iso_ref[...], d_i[:DFT_H].astype(BF16))
          - sign * d_i[DFT_H:DFT_H + 1])
    for c in range(n_chunks):
        lanes = slice(c * LANES, (c + 1) * LANES)
        y_ref[c, pl.ds(0, DFT_H, stride=2), :] = ye[:, lanes]
        y_ref[c, pl.ds(1, DFT_H, stride=2), :] = yo[:, lanes]
    for c in range(n_chunks):
        lanes = slice(c * LANES, (c + 1) * LANES)
        y = y_ref[c] + z_ref[0, c] * hb_ref[:, lanes]
        o_ref[0, :, lanes] = (y * x0_ref[0, :, lanes]).astype(BF16)


def hyena_conv(z, x0, h_bias, spectra, fwd_mats, inv_mats):
    b = z.shape[0]
    n_cc = HYENA_WIDTH // MXU_N
    spec_block = pl.BlockSpec((DFT_FP, MXU_N), lambda c, i: (0, c))
    return pl.pallas_call(
        _hyena_conv_kernel,
        out_shape=jax.ShapeDtypeStruct((b, SEQ, HYENA_WIDTH), BF16),
        grid=(n_cc, b),
        in_specs=[
            pl.BlockSpec((1, MXU_N // LANES, SEQ, LANES), lambda c, i: (i, c, 0, 0)),
            pl.BlockSpec((1, SEQ, MXU_N), lambda c, i: (i, 0, c)),
            pl.BlockSpec((1, MXU_N), lambda c, i: (0, c)),
            spec_block, spec_block, spec_block, spec_block,
        ] + [_resident(m.shape, 2) for m in fwd_mats + inv_mats],
        out_specs=pl.BlockSpec((1, SEQ, MXU_N), lambda c, i: (i, 0, c)),
        scratch_shapes=[pltpu.VMEM((MXU_N // LANES, SEQ, LANES), F32)],
        compiler_params=_params(2),
        name="hyena_conv",
    )(z, x0, h_bias, *spectra, *fwd_mats, *inv_mats)


def _head_rms(t, g):
    low = lax.broadcasted_iota(jnp.int32, (1, LANES), 1) < HEAD_DIM
    blocks = []
    for j in range(D_MODEL // LANES):
        blk = t[:, j * LANES:(j + 1) * LANES]
        sq = blk * blk
        s_lo = jnp.sum(jnp.where(low, sq, 0.0), axis=-1, keepdims=True)
        s_hi = jnp.sum(jnp.where(low, 0.0, sq), axis=-1, keepdims=True)
        ms = jnp.where(low, s_lo, s_hi) * (1.0 / HEAD_DIM)
        blocks.append(blk * lax.rsqrt(ms + EPS))
    return jnp.concatenate(blocks, axis=1) * g


def _qkv_kernel(x_ref, g_ref, w_ref, gq_ref, gk_ref, q_ref, k_ref, v_ref):
    xn = _rms_scale(x_ref[...], g_ref[...]).astype(BF16)
    q = _dot(xn, w_ref[:, 0:D_MODEL])
    q_ref[...] = (_head_rms(q, gq_ref[...]) * (HEAD_DIM ** -0.5)).astype(BF16)
    k = _dot(xn, w_ref[:, D_MODEL:2 * D_MODEL])
    k_ref[...] = _head_rms(k, gk_ref[...]).astype(BF16)
    v_ref[...] = _dot(xn, w_ref[:, 2 * D_MODEL:3 * D_MODEL]).astype(BF16)


def qkv_proj(x, g, w_qkv, gq, gk):
    n = x.shape[0]
    tile = pl.BlockSpec((FFN_TOKENS, D_MODEL), lambda i: (i, 0))
    out = jax.ShapeDtypeStruct((n, D_MODEL), BF16)
    return pl.pallas_call(
        _qkv_kernel,
        out_shape=(out, out, out),
        grid=(n // FFN_TOKENS,),
        in_specs=[tile, _resident((1, D_MODEL)), _resident(w_qkv.shape), _resident((1, D_MODEL)),
                  _resident((1, D_MODEL))],
        out_specs=(tile, tile, tile),
        compiler_params=_params(),
        name="qkv_proj",
    )(x, g, w_qkv, gq, gk)


def _att_group_start(gi):
    return min(max(ATT_QROWS * gi - NA_ROWS // 2, 0), ROWS - ATT_KROWS)


def _att_group_type(gi):
    if gi < 2:
        return gi
    if gi >= ATT_GROUPS - 2:
        return ATT_TYPES - (ATT_GROUPS - gi)
    return 2


def _att_row_offsets():
    dr = np.full((ATT_TYPES, ATT_QROWS, ATT_KROWS), ATT_DR_INVALID, np.int32)
    reps = {0: 0, 1: 1, 2: 2, 3: ATT_GROUPS - 2, 4: ATT_GROUPS - 1}
    for ty, gi in reps.items():
        start = _att_group_start(gi)
        for j in range(ATT_QROWS):
            qrow = ATT_QROWS * gi + j
            s_q = min(max(qrow - NA_ROWS // 2, 0), ROWS - NA_ROWS)
            for i in range(ATT_KROWS):
                krow = start + i
                if s_q <= krow < s_q + NA_ROWS:
                    dr[ty, j, i] = krow - qrow + NA_ROWS - 1
    return dr


def _attn_kernel(q_ref, k_ref, v_ref, tab_ref, o_ref, bias_ref, vt_ref, s_ref, p_ref):
    low = lax.broadcasted_iota(jnp.int32, (1, LANES), 1) < HEAD_DIM
    dr = _att_row_offsets()

    @pl.when(pl.program_id(1) == 0)
    def _build_bias():
        for ty in range(ATT_TYPES):
            for i in range(ATT_KROWS):
                for hh in range(2):
                    tile = jnp.where(low, tab_ref[hh, int(dr[ty, 0, i])], tab_ref[hh, int(dr[ty, 1, i])])
                    bias_ref[ty, i * GRID_W:(i + 1) * GRID_W, hh * LANES:(hh + 1) * LANES] = tile

    vt_ref[0:LANES, :] = v_ref[0].T
    vt_ref[LANES:LANES + ATT_ONES, :] = jnp.ones((ATT_ONES, SEQ), BF16)

    def scores(gi):
        q0 = gi * ATT_Q
        k0 = _att_group_start(gi) * GRID_W
        qg = q_ref[0, q0:q0 + ATT_Q, :]
        zero = jnp.zeros_like(qg)
        qs = jnp.concatenate([jnp.where(low, qg, zero), jnp.where(low, zero, qg)], axis=0)
        st = lax.dot_general(k_ref[0, k0:k0 + ATT_K, :], qs, (((1,), (1,)), ((), ())),
                             preferred_element_type=F32)
        s_ref[gi] = st + bias_ref[_att_group_type(gi)]

    def softmax(gi):
        s = s_ref[gi]
        m = jnp.max(s, axis=0, keepdims=True)
        p_ref[gi] = jnp.exp(s - m).astype(BF16)

    def values(gi):
        q0 = gi * ATT_Q
        k0 = _att_group_start(gi) * GRID_W
        r = _dot(vt_ref[:, k0:k0 + ATT_K], p_ref[gi])
        den = r[LANES:LANES + 1]
        o0 = r[0:HEAD_DIM, 0:LANES] / den[:, 0:LANES]
        o1 = r[HEAD_DIM:LANES, LANES:2 * LANES] / den[:, LANES:2 * LANES]
        o_ref[0, q0:q0 + ATT_Q, :] = jnp.concatenate([o0, o1], axis=0).T.astype(BF16)

    for step in range(ATT_GROUPS + 2):
        if step < ATT_GROUPS:
            scores(step)
        if 1 <= step <= ATT_GROUPS:
            softmax(step - 1)
        if step >= 2:
            values(step - 2)


def attention(q, k, v, table):
    b = q.shape[0]
    blk = pl.BlockSpec((1, SEQ, LANES), lambda h, i: (i, 0, h))
    return pl.pallas_call(
        _attn_kernel,
        out_shape=jax.ShapeDtypeStruct((b, SEQ, D_MODEL), BF16),
        grid=(N_HEADS // 2, b),
        in_specs=[blk, blk, blk,
                  pl.BlockSpec((2, ATT_DR_INVALID + 1, GRID_W, LANES), lambda h, i: (h, 0, 0, 0))],
        out_specs=blk,
        scratch_shapes=[pltpu.VMEM((ATT_TYPES, ATT_K, 2 * ATT_Q), F32),
                        pltpu.VMEM((LANES + ATT_ONES, SEQ), BF16),
                        pltpu.VMEM((ATT_GROUPS, ATT_K, 2 * ATT_Q), F32),
                        pltpu.VMEM((ATT_GROUPS, ATT_K, 2 * ATT_Q), BF16)],
        compiler_params=pltpu.CompilerParams(dimension_semantics=("arbitrary", "arbitrary"),
                                             vmem_limit_bytes=VMEM_LIMIT_BYTES),
        name="attention",
    )(q, k, v, table)


def _attn_bias_table(rpb):
    c = np.arange(GRID_W)
    c_start = np.clip(c - NA_COLS // 2, 0, GRID_W - NA_COLS)
    col_ok = (c[None, :] >= c_start[:, None]) & (c[None, :] < c_start[:, None] + NA_COLS)
    dc = np.clip(c[None, :] - c[:, None] + NA_COLS - 1, 0, 2 * NA_COLS - 2)
    onehot = (dc.T[None] == np.arange(2 * NA_COLS - 1)[:, None, None]).astype(np.float32)
    tab = jnp.einsum("hrd,dkq->hrkq", rpb.astype(F32), jnp.asarray(onehot), precision=lax.Precision.HIGHEST)
    tab = jnp.where(jnp.asarray(col_ok.T)[None, None], tab, NEG_BIG)
    tab = jnp.concatenate([tab, jnp.full((N_HEADS, 1, GRID_W, GRID_W), NEG_BIG, F32)], axis=1)
    return jnp.concatenate([tab, tab], axis=-1)


def _block_diag_pairs(pool_w):
    z = jnp.zeros((LANES, LANES), pool_w.dtype)
    pairs = []
    for pp in range(2):
        top = jnp.concatenate([pool_w[2 * pp], z], axis=1)
        bot = jnp.concatenate([z, pool_w[2 * pp + 1]], axis=1)
        pairs.append(jnp.concatenate([top, bot], axis=0))
    return jnp.stack(pairs)


def kernel(x_prompt, x_sample, norm_g, ffn_w_gate, ffn_w_up, ffn_w_down, w_in_ab, pool_w, pool_scale, short_w, short_b, filt_w1, filt_b1, filt_w2, filt_b2, filt_w3, filt_b3, filt_freq, filt_w_out, filt_deltas, hyena_bias, w_out_ab, w_qkv, q_norm_g, k_norm_g, rpb, w_o):
    def ffn_weights(layer, j):
        return (norm_g[layer, 2 * j][None], ffn_w_gate[layer, j].astype(BF16), ffn_w_up[layer, j].astype(BF16),
                ffn_w_down[layer, j].astype(BF16))

    ffn_w = [[ffn_weights(layer, j) for j in range(2)] for layer in range(2)]
    w_in = w_in_ab[0].astype(BF16)
    pool_w2 = _block_diag_pairs(pool_w[0]).astype(BF16)
    w_qkv_b = w_qkv[0].astype(BF16)
    w_o_b = w_o[0].astype(BF16)
    gq = jnp.tile(q_norm_g[0], N_HEADS)[None]
    gk = jnp.tile(k_norm_g[0], N_HEADS)[None]

    fwd_np, inv_np = _dft_matrices()
    fwd_mats = [jnp.asarray(m).astype(BF16) for m in fwd_np]
    inv_mats = [jnp.asarray(m).astype(BF16) for m in inv_np]
    emb_np, tnorm_np = _filter_embedding()
    w1 = jnp.zeros((LANES, HYENA_HIDDEN), BF16).at[:HYENA_EMB].set(filt_w1[0].astype(BF16))
    spectra = hyena_filter(jnp.asarray(emb_np).astype(F32), jnp.asarray(tnorm_np).astype(F32), w1, filt_b1[0][None],
                           filt_w2[0].astype(BF16), filt_b2[0][None], filt_w3[0].astype(BF16), filt_b3[0][None],
                           filt_freq[0][None], filt_w_out[0].astype(BF16), filt_deltas[0][None], fwd_mats)
    table = _attn_bias_table(rpb[0])
    w_out_a = w_out_ab[0, :POOL_WIDTH].astype(BF16)
    w_out_h = w_out_ab[0, POOL_WIDTH:].astype(BF16)

    outs = []
    for x in (x_prompt, x_sample):
        b = x.shape[0]
        n = b * SEQ
        t = ffn(x.reshape(n, D_MODEL), *ffn_w[0][0])
        a, z, x0 = mix0_in(t.reshape(b, SEQ, D_MODEL), norm_g[0, 1][None], w_in, pool_w2, pool_scale[0][None],
                           short_w[0], short_b[0][None])
        yh = hyena_conv(z, x0, hyena_bias[0][None], spectra, fwd_mats, inv_mats)
        t = ffn(t, *ffn_w[0][1], pre=((a.reshape(n, POOL_WIDTH), w_out_a), (yh.reshape(n, HYENA_WIDTH), w_out_h)))
        t = ffn(t, *ffn_w[1][0])
        q, k, v = qkv_proj(t, norm_g[1, 1][None], w_qkv_b, gq, gk)
        o = attention(q.reshape(b, SEQ, D_MODEL), k.reshape(b, SEQ, D_MODEL), v.reshape(b, SEQ, D_MODEL), table)
        t = ffn(t, *ffn_w[1][1], pre=((o.reshape(n, D_MODEL), w_o_b),))
        outs.append(t.reshape(b, SEQ, D_MODEL))
    return tuple(outs)
```

```python
import functools
import math

import jax
import jax.numpy as jnp
import numpy as np
from jax import lax
from jax.experimental import pallas as pl
from jax.experimental.pallas import tpu as pltpu

BF16 = jnp.bfloat16
F32 = jnp.float32

D_MODEL = 1024
SEQ = 2048
D_FF = 2816
EPS = 1e-6
GRID_W = 64
POOL_WIDTH = 512
POOL_WINDOWS = (2, 4, 8, 16)
HYENA_WIDTH = 512
HYENA_BANDS = 16
HYENA_EMB = 1 + 2 * HYENA_BANDS
HYENA_HIDDEN = 64
HYENA_MOD_SHIFT = 0.05
N_HEADS = 16
HEAD_DIM = 64
NA_ROWS = 8
NA_COLS = 16
ROWS = SEQ // GRID_W

LANES = 128
MXU_N = 256
VMEM_LIMIT_BYTES = 56 * 1024 * 1024

FFN_TOKENS = 512
FFN_SUB = 2
FFN_CHUNK = MXU_N

DFT_N = 2 * SEQ
DFT_H = SEQ // 2
DFT_FP = DFT_H + 8
POOL_PAD = 8

ATT_QROWS = 2
ATT_KROWS = 10
ATT_Q = ATT_QROWS * GRID_W
ATT_K = ATT_KROWS * GRID_W
ATT_GROUPS = ROWS // ATT_QROWS
ATT_TYPES = 5
ATT_DR_INVALID = 2 * NA_ROWS - 1
ATT_ONES = 16
NEG_BIG = -1e30
LOG2_E = math.log2(math.e)
ATT_Q_SCALE = HEAD_DIM ** -0.5 * LOG2_E


def _params(n_axes=1):
    return pltpu.CompilerParams(dimension_semantics=("parallel",) * n_axes,
                                vmem_limit_bytes=VMEM_LIMIT_BYTES)


def _resident(shape, n_axes=1):
    zeros = (0,) * len(shape)
    if n_axes == 1:
        return pl.BlockSpec(shape, lambda i: zeros, pipeline_mode=pl.Buffered(1))
    return pl.BlockSpec(shape, lambda i, j: zeros, pipeline_mode=pl.Buffered(1))


def _dot(a, b):
    return jnp.dot(a, b, preferred_element_type=F32)


def _rms_scale(x, g):
    ms = jnp.mean(x * x, axis=-1, keepdims=True)
    return x * lax.rsqrt(ms + EPS) * g


def _ffn_kernel(n_pre, x_ref, *refs):
    pre = refs[:2 * n_pre]
    g_ref, wg_ref, wu_ref, wd_ref, o_ref = refs[2 * n_pre:]
    for sub in range(FFN_SUB):
        rows = slice(sub * FFN_TOKENS, (sub + 1) * FFN_TOKENS)
        x = x_ref[rows, :]
        for i in range(n_pre):
            x = x + _dot(pre[2 * i][rows, :], pre[2 * i + 1][...])
        xn = _rms_scale(x, g_ref[...]).astype(BF16)
        acc = jnp.zeros(x.shape, F32)
        for c in range(D_FF // FFN_CHUNK):
            cols = slice(c * FFN_CHUNK, (c + 1) * FFN_CHUNK)
            gate = _dot(xn, wg_ref[:, cols])
            up = _dot(xn, wu_ref[:, cols])
            h = (gate / (1.0 + jnp.exp(-gate)) * up).astype(BF16)
            acc = acc + _dot(h, wd_ref[cols, :])
        o_ref[rows, :] = x + 0.5 * acc


def _layer_slot(shape, layer, slot):
    return pl.BlockSpec((None, None) + shape, lambda i: (layer, slot, 0, 0), pipeline_mode=pl.Buffered(1))


def ffn(x, g, wg, wu, wd, layer, slot, pre=()):
    n = x.shape[0]
    step = FFN_SUB * FFN_TOKENS
    tile = pl.BlockSpec((step, D_MODEL), lambda i: (i, 0))
    in_specs, args = [tile], [x]
    for act, w in pre:
        in_specs += [pl.BlockSpec((step, act.shape[1]), lambda i: (i, 0)), _resident(w.shape)]
        args += [act, w]
    in_specs += [_resident((1, D_MODEL)), _layer_slot((D_MODEL, D_FF), layer, slot),
                 _layer_slot((D_MODEL, D_FF), layer, slot), _layer_slot((D_FF, D_MODEL), layer, slot)]
    args += [g, wg, wu, wd]
    return pl.pallas_call(
        functools.partial(_ffn_kernel, len(pre)),
        out_shape=jax.ShapeDtypeStruct(x.shape, F32),
        grid=(n // step,),
        in_specs=in_specs,
        out_specs=tile,
        compiler_params=_params(),
        name="ffn",
    )(*args)


def _shift_rows(x, k):
    return pltpu.roll(x, k % x.shape[0], 0)


def _window_sum(ue, w):
    p = ue + _shift_rows(ue, 1)
    if w == 2:
        return p
    span = 2
    while 2 * span < w:
        p = p + _shift_rows(p, span)
        span *= 2
    return _shift_rows(p, 1) + _shift_rows(p, -(span - 1))


def _mix0_in_kernel(x_ref, g_ref, win_ref, pw_ref, ps_ref, sw_ref, sb_ref, a_ref, z_ref, x0_ref, pad_ref):
    xn = _rms_scale(x_ref[0], g_ref[...]).astype(BF16)
    row8 = lax.broadcasted_iota(jnp.int32, (POOL_PAD, 1), 0)

    zeros = jnp.zeros((POOL_PAD, MXU_N), F32)
    pad_ref[0:POOL_PAD, :] = zeros
    pad_ref[POOL_PAD + SEQ:POOL_PAD + SEQ + POOL_PAD, :] = zeros
    for pp in range(POOL_WIDTH // MXU_N):
        u = _dot(xn, win_ref[:, pp * MXU_N:(pp + 1) * MXU_N])
        pad_ref[POOL_PAD:POOL_PAD + SEQ, :] = u
        ue = pad_ref[...]
        halves = []
        for j in range(MXU_N // LANES):
            w = POOL_WINDOWS[pp * (MXU_N // LANES) + j]
            lanes = slice(j * LANES, (j + 1) * LANES)
            win = _window_sum(ue[:, lanes], w)[POOL_PAD:POOL_PAD + SEQ]
            uj = u[:, lanes]

            def edge(r0, win=win, uj=uj, w=w):
                rr = r0 + row8
                cnt = (jnp.minimum(rr + (w - w // 2 - 1), SEQ - 1) - jnp.maximum(rr - w // 2, 0) + 1).astype(F32)
                return win[r0:r0 + POOL_PAD] / cnt - uj[r0:r0 + POOL_PAD]

            inner = win[POOL_PAD:SEQ - POOL_PAD] * (1.0 / w) - uj[POOL_PAD:SEQ - POOL_PAD]
            halves.append(jnp.concatenate([edge(0), inner, edge(SEQ - POOL_PAD)], axis=0))
        pooled = jnp.concatenate(halves, axis=1).astype(BF16)
        mixed = _dot(pooled, pw_ref[pp]) * ps_ref[:, pp * MXU_N:(pp + 1) * MXU_N]
        a_ref[0, :, pp * MXU_N:(pp + 1) * MXU_N] = mixed.astype(BF16)

    first_row = row8 == 0
    last_row = row8 == POOL_PAD - 1
    for cc in range(HYENA_WIDTH // MXU_N):
        parts = []
        for part in range(3):
            c0 = part * HYENA_WIDTH + cc * MXU_N
            p = _dot(xn, win_ref[:, POOL_WIDTH + c0:POOL_WIDTH + c0 + MXU_N])
            w = sw_ref[:, c0:c0 + MXU_N]
            bias = sb_ref[:, c0:c0 + MXU_N]
            prev = _shift_rows(p, 1)
            nxt = _shift_rows(p, -1)
            conv = prev * w[0:1] + p * w[1:2] + nxt * w[2:3] + bias
            top = slice(0, POOL_PAD)
            bot = slice(SEQ - POOL_PAD, SEQ)
            conv_top = jnp.where(first_row, 0.0, prev[top]) * w[0:1] + p[top] * w[1:2] + nxt[top] * w[2:3] + bias
            conv_bot = prev[bot] * w[0:1] + p[bot] * w[1:2] + jnp.where(last_row, 0.0, nxt[bot]) * w[2:3] + bias
            parts.append(jnp.concatenate([conv_top, conv[POOL_PAD:SEQ - POOL_PAD], conv_bot], axis=0))
        x0c, x1c, vc = parts
        zc = vc * x1c
        for j in range(MXU_N // LANES):
            z_ref[0, cc * (MXU_N // LANES) + j] = zc[:, j * LANES:(j + 1) * LANES]
        x0_ref[0, :, cc * MXU_N:(cc + 1) * MXU_N] = x0c


def mix0_in(x, g, w_in, pool_w2, pool_scale, short_w, short_b):
    b = x.shape[0]
    return pl.pallas_call(
        _mix0_in_kernel,
        out_shape=(jax.ShapeDtypeStruct((b, SEQ, POOL_WIDTH), BF16),
                   jax.ShapeDtypeStruct((b, HYENA_WIDTH // LANES, SEQ, LANES), F32),
                   jax.ShapeDtypeStruct((b, SEQ, HYENA_WIDTH), F32)),
        grid=(b,),
        in_specs=[
            pl.BlockSpec((1, SEQ, D_MODEL), lambda i: (i, 0, 0)),
            _resident((1, D_MODEL)),
            _resident(w_in.shape),
            _resident(pool_w2.shape),
            _resident((1, POOL_WIDTH)),
            _resident(short_w.shape),
            _resident(short_b.shape),
        ],
        out_specs=(pl.BlockSpec((1, SEQ, POOL_WIDTH), lambda i: (i, 0, 0)),
                   pl.BlockSpec((1, HYENA_WIDTH // LANES, SEQ, LANES), lambda i: (i, 0, 0, 0)),
                   pl.BlockSpec((1, SEQ, HYENA_WIDTH), lambda i: (i, 0, 0))),
        scratch_shapes=[pltpu.VMEM((SEQ + 2 * POOL_PAD, MXU_N), F32)],
        compiler_params=_params(),
        name="mix0_in",
    )(x, g, w_in, pool_w2, pool_scale, short_w, short_b)


def _dft_matrices():
    f = np.arange(DFT_FP, dtype=np.int64)[:, None]
    s = np.arange(DFT_H, dtype=np.int64)[None, :]
    keep = f <= DFT_H
    unit = 2.0 * np.pi / DFT_N
    ang_e = unit * ((f * (2 * s)) % DFT_N)
    ang_o = unit * ((f * (2 * s + 1)) % DFT_N)
    fwd = [np.where(keep, m, 0.0) for m in (np.cos(ang_e), np.sin(ang_e), np.cos(ang_o), np.sin(ang_o))]
    t = np.arange(DFT_H, dtype=np.int64)[:, None]
    fi = np.arange(DFT_H, dtype=np.int64)[None, :]
    wgt = np.where(fi == 0, 1.0, 2.0) / DFT_N
    ph_e = unit * ((fi * (2 * t)) % DFT_N)
    ph_o = unit * ((fi * (2 * t + 1)) % DFT_N)
    inv = [wgt * np.cos(ph_e), -wgt * np.sin(ph_e), wgt * np.cos(ph_o), -wgt * np.sin(ph_o)]
    return fwd, inv


def _filter_embedding():
    t_idx = np.arange(SEQ, dtype=np.float64)
    t_norm = np.linspace(0.0, 1.0, SEQ, dtype=np.float64)
    bands = np.linspace(1e-4, HYENA_BANDS - 1, HYENA_BANDS, dtype=np.float64)
    ang = (2.0 * math.pi / SEQ) * t_idx[:, None] * bands[None, :]
    out = np.zeros((SEQ, LANES), np.float64)
    out[:, :HYENA_EMB] = np.concatenate([t_norm[:, None], np.cos(ang), -np.sin(ang)], axis=-1)
    return out, t_norm[:, None]


def _strided_rows(ref, chunk, parity):
    return ref[chunk, pl.ds(parity, DFT_H, stride=2), :]


def _filter_kernel(emb_ref, tn_ref, w1_ref, b1_ref, w2_ref, b2_ref, w3_ref, b3_ref, fr_ref, wo_ref, dl_ref,
                   ce_ref, se_ref, co_ref, so_ref, kpr_ref, kpi_ref, kmr_ref, kmi_ref, s_ref, d_ref):
    fr = fr_ref[...]
    h = jnp.sin(fr * (_dot(emb_ref[...].astype(BF16), w1_ref[...]) + b1_ref[...]))
    h = jnp.sin(fr * (_dot(h.astype(BF16), w2_ref[...]) + b2_ref[...]))
    h = jnp.sin(fr * (_dot(h.astype(BF16), w3_ref[...]) + b3_ref[...]))
    ho = _dot(h.astype(BF16), wo_ref[...])
    decay = jnp.exp(-tn_ref[...] * jnp.abs(dl_ref[...]))
    hm = ho * (decay + HYENA_MOD_SHIFT)
    row = lax.broadcasted_iota(jnp.int32, (SEQ, 1), 0)
    fwd = hm[:, :HYENA_WIDTH]
    bwd = jnp.where(row == 0, 0.0, hm[:, HYENA_WIDTH:])
    ssum = fwd + bwd
    sdif = bwd - fwd
    n_chunks = HYENA_WIDTH // LANES
    for c in range(n_chunks):
        s_ref[c] = ssum[:, c * LANES:(c + 1) * LANES]
        d_ref[c] = sdif[:, c * LANES:(c + 1) * LANES]

    def parity(ref, par):
        return jnp.concatenate([_strided_rows(ref, c, par) for c in range(n_chunks)], axis=1).astype(BF16)

    kpr_ref[...] = 2.0 * _dot(ce_ref[...], parity(s_ref, 0))
    kpi_ref[...] = 2.0 * _dot(se_ref[...], parity(d_ref, 0))
    kmr_ref[...] = 2.0 * _dot(co_ref[...], parity(s_ref, 1))
    kmi_ref[...] = 2.0 * _dot(so_ref[...], parity(d_ref, 1))


def hyena_filter(emb, tnorm, w1, b1, w2, b2, w3, b3, freq, w_out, deltas, fwd_mats):
    spec = jax.ShapeDtypeStruct((DFT_FP, HYENA_WIDTH), F32)
    return pl.pallas_call(
        _filter_kernel,
        out_shape=(spec,) * 4,
        scratch_shapes=[pltpu.VMEM((HYENA_WIDTH // LANES, SEQ, LANES), F32)] * 2,
        compiler_params=pltpu.CompilerParams(vmem_limit_bytes=VMEM_LIMIT_BYTES),
        name="hyena_filter",
    )(emb, tnorm, w1, b1, w2, b2, w3, b3, freq, w_out, deltas, *fwd_mats)


def _hyena_conv_kernel(z_ref, x0_ref, hb_ref, kpr_ref, kpi_ref, kmr_ref, kmi_ref,
                       ce_ref, se_ref, co_ref, so_ref, ice_ref, ise_ref, ico_ref, iso_ref, o_ref, y_ref):
    n_chunks = MXU_N // LANES
    ze = jnp.concatenate([_strided_rows(z_ref.at[0], c, 0) for c in range(n_chunks)], axis=1).astype(BF16)
    zo = jnp.concatenate([_strided_rows(z_ref.at[0], c, 1) for c in range(n_chunks)], axis=1).astype(BF16)
    er = _dot(ce_ref[...], ze)
    ei = _dot(se_ref[...], ze)
    orr = _dot(co_ref[...], zo)
    oi = _dot(so_ref[...], zo)
    kpr, kpi, kmr, kmi = kpr_ref[...], kpi_ref[...], kmr_ref[...], kmi_ref[...]
    a_r = er * kpr + ei * kpi + orr * kmr + oi * kmi
    a_i = er * kpi - ei * kpr + orr * kmi - oi * kmr
    d_r = er * kmr + ei * kmi + orr * kpr + oi * kpi
    d_i = er * kmi - ei * kmr + orr * kpi - oi * kpr
    trow = lax.broadcasted_iota(jnp.int32, (DFT_H, 1), 0)
    sign = jnp.where((trow & 1) == 0, 1.0 / DFT_N, -1.0 / DFT_N)
    ye = (_dot(ice_ref[...], a_r[:DFT_H].astype(BF16)) + _dot(ise_ref[...], a_i[:DFT_H].astype(BF16))
          + sign * a_r[DFT_H:DFT_H + 1])
    yo = (_dot(ico_ref[...], d_r[:DFT_H].astype(BF16)) + _dot(iso_ref[...], d_i[:DFT_H].astype(BF16))
          - sign * d_i[DFT_H:DFT_H + 1])
    for c in range(n_chunks):
        lanes = slice(c * LANES, (c + 1) * LANES)
        y_ref[c, pl.ds(0, DFT_H, stride=2), :] = ye[:, lanes]
        y_ref[c, pl.ds(1, DFT_H, stride=2), :] = yo[:, lanes]
    for c in range(n_chunks):
        lanes = slice(c * LANES, (c + 1) * LANES)
        y = y_ref[c] + z_ref[0, c] * hb_ref[:, lanes]
        o_ref[0, :, lanes] = (y * x0_ref[0, :, lanes]).astype(BF16)


def hyena_conv(z, x0, h_bias, spectra, fwd_mats, inv_mats):
    b = z.shape[0]
    n_cc = HYENA_WIDTH // MXU_N
    spec_block = pl.BlockSpec((DFT_FP, MXU_N), lambda c, i: (0, c))
    return pl.pallas_call(
        _hyena_conv_kernel,
        out_shape=jax.ShapeDtypeStruct((b, SEQ, HYENA_WIDTH), BF16),
        grid=(n_cc, b),
        in_specs=[
            pl.BlockSpec((1, MXU_N // LANES, SEQ, LANES), lambda c, i: (i, c, 0, 0)),
            pl.BlockSpec((1, SEQ, MXU_N), lambda c, i: (i, 0, c)),
            pl.BlockSpec((1, MXU_N), lambda c, i: (0, c)),
            spec_block, spec_block, spec_block, spec_block,
        ] + [_resident(m.shape, 2) for m in fwd_mats + inv_mats],
        out_specs=pl.BlockSpec((1, SEQ, MXU_N), lambda c, i: (i, 0, c)),
        scratch_shapes=[pltpu.VMEM((MXU_N // LANES, SEQ, LANES), F32)],
        compiler_params=_params(2),
        name="hyena_conv",
    )(z, x0, h_bias, *spectra, *fwd_mats, *inv_mats)


def _head_rms(t, g):
    low = lax.broadcasted_iota(jnp.int32, (1, LANES), 1) < HEAD_DIM
    blocks = []
    for j in range(D_MODEL // LANES):
        blk = t[:, j * LANES:(j + 1) * LANES]
        sq = blk * blk
        s_lo = jnp.sum(jnp.where(low, sq, 0.0), axis=-1, keepdims=True)
        s_hi = jnp.sum(jnp.where(low, 0.0, sq), axis=-1, keepdims=True)
        ms = jnp.where(low, s_lo, s_hi) * (1.0 / HEAD_DIM)
        blocks.append(blk * lax.rsqrt(ms + EPS))
    return jnp.concatenate(blocks, axis=1) * g


def _qkv_kernel(x_ref, g_ref, w_ref, gq_ref, gk_ref, q_ref, k_ref, v_ref):
    for sub in range(FFN_SUB):
        rows = slice(sub * FFN_TOKENS, (sub + 1) * FFN_TOKENS)
        xn = _rms_scale(x_ref[rows, :], g_ref[...]).astype(BF16)
        q = _dot(xn, w_ref[:, 0:D_MODEL])
        q_ref[rows, :] = (_head_rms(q, gq_ref[...]) * ATT_Q_SCALE).astype(BF16)
        k = _dot(xn, w_ref[:, D_MODEL:2 * D_MODEL])
        k_ref[rows, :] = _head_rms(k, gk_ref[...]).astype(BF16)
        v_ref[rows, :] = _dot(xn, w_ref[:, 2 * D_MODEL:3 * D_MODEL]).astype(BF16)


def qkv_proj(x, g, w_qkv, gq, gk):
    n = x.shape[0]
    step = FFN_SUB * FFN_TOKENS
    tile = pl.BlockSpec((step, D_MODEL), lambda i: (i, 0))
    out = jax.ShapeDtypeStruct((n, D_MODEL), BF16)
    return pl.pallas_call(
        _qkv_kernel,
        out_shape=(out, out, out),
        grid=(n // step,),
        in_specs=[tile, _resident((1, D_MODEL)), _resident(w_qkv.shape), _resident((1, D_MODEL)),
                  _resident((1, D_MODEL))],
        out_specs=(tile, tile, tile),
        compiler_params=_params(),
        name="qkv_proj",
    )(x, g, w_qkv, gq, gk)


def _att_group_start(gi):
    return min(max(ATT_QROWS * gi - NA_ROWS // 2, 0), ROWS - ATT_KROWS)


def _att_group_type(gi):
    if gi < 2:
        return gi
    if gi >= ATT_GROUPS - 2:
        return ATT_TYPES - (ATT_GROUPS - gi)
    return 2


def _att_row_offsets():
    dr = np.full((ATT_TYPES, ATT_QROWS, ATT_KROWS), ATT_DR_INVALID, np.int32)
    reps = {0: 0, 1: 1, 2: 2, 3: ATT_GROUPS - 2, 4: ATT_GROUPS - 1}
    for ty, gi in reps.items():
        start = _att_group_start(gi)
        for j in range(ATT_QROWS):
            qrow = ATT_QROWS * gi + j
            s_q = min(max(qrow - NA_ROWS // 2, 0), ROWS - NA_ROWS)
            for i in range(ATT_KROWS):
                krow = start + i
                if s_q <= krow < s_q + NA_ROWS:
                    dr[ty, j, i] = krow - qrow + NA_ROWS - 1
    return dr


def _attn_kernel(q_ref, k_ref, v_ref, tab_ref, o_ref, bias_ref, vt_ref, s_ref, p_ref):
    low = lax.broadcasted_iota(jnp.int32, (1, LANES), 1) < HEAD_DIM
    dr = _att_row_offsets()

    @pl.when(pl.program_id(1) == 0)
    def _build_bias():
        for ty in range(ATT_TYPES):
            for i in range(ATT_KROWS):
                for hh in range(2):
                    tile = jnp.where(low, tab_ref[hh, int(dr[ty, 0, i])], tab_ref[hh, int(dr[ty, 1, i])])
                    bias_ref[ty, i * GRID_W:(i + 1) * GRID_W, hh * LANES:(hh + 1) * LANES] = tile

    vt_ref[0:LANES, :] = v_ref[0].T
    vt_ref[LANES:LANES + ATT_ONES, :] = jnp.ones((ATT_ONES, SEQ), BF16)

    def scores(gi):
        q0 = gi * ATT_Q
        k0 = _att_group_start(gi) * GRID_W
        qg = q_ref[0, q0:q0 + ATT_Q, :]
        zero = jnp.zeros_like(qg)
        qs = jnp.concatenate([jnp.where(low, qg, zero), jnp.where(low, zero, qg)], axis=0)
        st = lax.dot_general(k_ref[0, k0:k0 + ATT_K, :], qs, (((1,), (1,)), ((), ())),
                             preferred_element_type=F32)
        s_ref[gi] = st + bias_ref[_att_group_type(gi)]

    def softmax(gi):
        s = s_ref[gi]
        m = jnp.max(s, axis=0, keepdims=True)
        p_ref[gi] = jnp.exp2(s - m).astype(BF16)

    def values(gi):
        q0 = gi * ATT_Q
        k0 = _att_group_start(gi) * GRID_W
        r = _dot(vt_ref[:, k0:k0 + ATT_K], p_ref[gi])
        den = r[LANES:LANES + 1]
        o0 = r[0:HEAD_DIM, 0:LANES] / den[:, 0:LANES]
        o1 = r[HEAD_DIM:LANES, LANES:2 * LANES] / den[:, LANES:2 * LANES]
        o_ref[0, q0:q0 + ATT_Q, :] = jnp.concatenate([o0, o1], axis=0).T.astype(BF16)

    for step in range(ATT_GROUPS + 2):
        if step < ATT_GROUPS:
            scores(step)
        if 1 <= step <= ATT_GROUPS:
            softmax(step - 1)
        if step >= 2:
            values(step - 2)


def attention(q, k, v, table):
    b = q.shape[0]
    blk = pl.BlockSpec((1, SEQ, LANES), lambda h, i: (i, 0, h))
    return pl.pallas_call(
        _attn_kernel,
        out_shape=jax.ShapeDtypeStruct((b, SEQ, D_MODEL), BF16),
        grid=(N_HEADS // 2, b),
        in_specs=[blk, blk, blk,
                  pl.BlockSpec((2, ATT_DR_INVALID + 1, GRID_W, LANES), lambda h, i: (h, 0, 0, 0))],
        out_specs=blk,
        scratch_shapes=[pltpu.VMEM((ATT_TYPES, ATT_K, 2 * ATT_Q), F32),
                        pltpu.VMEM((LANES + ATT_ONES, SEQ), BF16),
                        pltpu.VMEM((ATT_GROUPS, ATT_K, 2 * ATT_Q), F32),
                        pltpu.VMEM((ATT_GROUPS, ATT_K, 2 * ATT_Q), BF16)],
        compiler_params=pltpu.CompilerParams(dimension_semantics=("arbitrary", "arbitrary"),
                                             vmem_limit_bytes=VMEM_LIMIT_BYTES),
        name="attention",
    )(q, k, v, table)


def _attn_bias_table(rpb):
    c = np.arange(GRID_W)
    c_start = np.clip(c - NA_COLS // 2, 0, GRID_W - NA_COLS)
    col_ok = (c[None, :] >= c_start[:, None]) & (c[None, :] < c_start[:, None] + NA_COLS)
    dc = np.clip(c[None, :] - c[:, None] + NA_COLS - 1, 0, 2 * NA_COLS - 2)
    onehot = (dc.T[None] == np.arange(2 * NA_COLS - 1)[:, None, None]).astype(np.float32)
    tab = jnp.einsum("hrd,dkq->hrkq", rpb.astype(F32), jnp.asarray(onehot), precision=lax.Precision.HIGHEST)
    tab = jnp.where(jnp.asarray(col_ok.T)[None, None], tab * LOG2_E, NEG_BIG)
    tab = jnp.concatenate([tab, jnp.full((N_HEADS, 1, GRID_W, GRID_W), NEG_BIG, F32)], axis=1)
    return jnp.concatenate([tab, tab], axis=-1)


def _block_diag_pairs(pool_w):
    z = jnp.zeros((LANES, LANES), pool_w.dtype)
    pairs = []
    for pp in range(2):
        top = jnp.concatenate([pool_w[2 * pp], z], axis=1)
        bot = jnp.concatenate([z, pool_w[2 * pp + 1]], axis=1)
        pairs.append(jnp.concatenate([top, bot], axis=0))
    return jnp.stack(pairs)


def kernel(x_prompt, x_sample, norm_g, ffn_w_gate, ffn_w_up, ffn_w_down, w_in_ab, pool_w, pool_scale, short_w, short_b, filt_w1, filt_b1, filt_w2, filt_b2, filt_w3, filt_b3, filt_freq, filt_w_out, filt_deltas, hyena_bias, w_out_ab, w_qkv, q_norm_g, k_norm_g, rpb, w_o):
    wg = ffn_w_gate.astype(BF16)
    wu = ffn_w_up.astype(BF16)
    wd = ffn_w_down.astype(BF16)

    def run_ffn(t, layer, slot, pre=()):
        return ffn(t, norm_g[layer, 2 * slot][None], wg, wu, wd, layer, slot, pre)

    w_in = w_in_ab[0].astype(BF16)
    pool_w2 = _block_diag_pairs(pool_w[0]).astype(BF16)
    w_qkv_b = w_qkv[0].astype(BF16)
    w_o_b = w_o[0].astype(BF16)
    gq = jnp.tile(q_norm_g[0], N_HEADS)[None]
    gk = jnp.tile(k_norm_g[0], N_HEADS)[None]

    fwd_np, inv_np = _dft_matrices()
    fwd_mats = [jnp.asarray(m).astype(BF16) for m in fwd_np]
    inv_mats = [jnp.asarray(m).astype(BF16) for m in inv_np]
    emb_np, tnorm_np = _filter_embedding()
    w1 = jnp.zeros((LANES, HYENA_HIDDEN), BF16).at[:HYENA_EMB].set(filt_w1[0].astype(BF16))
    spectra = hyena_filter(jnp.asarray(emb_np).astype(F32), jnp.asarray(tnorm_np).astype(F32), w1, filt_b1[0][None],
                           filt_w2[0].astype(BF16), filt_b2[0][None], filt_w3[0].astype(BF16), filt_b3[0][None],
                           filt_freq[0][None], filt_w_out[0].astype(BF16), filt_deltas[0][None], fwd_mats)
    table = _attn_bias_table(rpb[0])
    w_out_a = w_out_ab[0, :POOL_WIDTH].astype(BF16)
    w_out_h = w_out_ab[0, POOL_WIDTH:].astype(BF16)

    outs = []
    for x in (x_prompt, x_sample):
        b = x.shape[0]
        n = b * SEQ
        t = run_ffn(x.reshape(n, D_MODEL), 0, 0)
        a, z, x0 = mix0_in(t.reshape(b, SEQ, D_MODEL), norm_g[0, 1][None], w_in, pool_w2, pool_scale[0][None],
                           short_w[0], short_b[0][None])
        yh = hyena_conv(z, x0, hyena_bias[0][None], spectra, fwd_mats, inv_mats)
        t = run_ffn(t, 0, 1, ((a.reshape(n, POOL_WIDTH), w_out_a), (yh.reshape(n, HYENA_WIDTH), w_out_h)))
        t = run_ffn(t, 1, 0)
        q, k, v = qkv_proj(t, norm_g[1, 1][None], w_qkv_b, gq, gk)
        o = attention(q.reshape(b, SEQ, D_MODEL), k.reshape(b, SEQ, D_MODEL), v.reshape(b, SEQ, D_MODEL), table)
        t = run_ffn(t, 1, 1, ((o.reshape(n, D_MODEL), w_o_b),))
        outs.append(t.reshape(b, SEQ, D_MODEL))
    return tuple(outs)
```

```python
import functools
import math

import jax
import jax.numpy as jnp
import numpy as np
from jax import lax
from jax.experimental import pallas as pl
from jax.experimental.pallas import tpu as pltpu

BF16 = jnp.bfloat16
F32 = jnp.float32

D_MODEL = 1024
SEQ = 2048
D_FF = 2816
EPS = 1e-6
GRID_W = 64
POOL_WIDTH = 512
POOL_WINDOWS = (2, 4, 8, 16)
HYENA_WIDTH = 512
HYENA_BANDS = 16
HYENA_EMB = 1 + 2 * HYENA_BANDS
HYENA_HIDDEN = 64
HYENA_MOD_SHIFT = 0.05
N_HEADS = 16
HEAD_DIM = 64
NA_ROWS = 8
NA_COLS = 16
ROWS = SEQ // GRID_W

LANES = 128
MXU_N = 256
VMEM_LIMIT_BYTES = 56 * 1024 * 1024

FFN_TOKENS = 512
FFN_SUB = 2
FFN_CHUNK = MXU_N

HY_BLOCKS = 2
HY_T = SEQ // HY_BLOCKS
HY_FILTERS = 2 * HY_BLOCKS - 1
DFT_N = 2 * HY_T
DFT_H = HY_T // 2
DFT_FP = DFT_H + 8
POOL_PAD = 8

ATT_QROWS = 2
ATT_KROWS = 10
ATT_Q = ATT_QROWS * GRID_W
ATT_K = ATT_KROWS * GRID_W
ATT_GROUPS = ROWS // ATT_QROWS
ATT_TYPES = 5
ATT_DR_INVALID = 2 * NA_ROWS - 1
ATT_ONES = 16
NEG_BIG = -1e30
LOG2_E = math.log2(math.e)
ATT_Q_SCALE = HEAD_DIM ** -0.5 * LOG2_E


def _params(n_axes=1):
    return pltpu.CompilerParams(dimension_semantics=("parallel",) * n_axes,
                                vmem_limit_bytes=VMEM_LIMIT_BYTES)


def _resident(shape, n_axes=1):
    zeros = (0,) * len(shape)
    if n_axes == 1:
        return pl.BlockSpec(shape, lambda i: zeros, pipeline_mode=pl.Buffered(1))
    return pl.BlockSpec(shape, lambda i, j: zeros, pipeline_mode=pl.Buffered(1))


def _dot(a, b):
    return jnp.dot(a, b, preferred_element_type=F32)


def _rms_scale(x, g):
    ms = jnp.mean(x * x, axis=-1, keepdims=True)
    return x * lax.rsqrt(ms + EPS) * g


def _ffn_kernel(n_pre, x_ref, *refs):
    pre = refs[:2 * n_pre]
    g_ref, wg_ref, wu_ref, wd_ref, o_ref = refs[2 * n_pre:]
    for sub in range(FFN_SUB):
        rows = slice(sub * FFN_TOKENS, (sub + 1) * FFN_TOKENS)
        x = x_ref[rows, :]
        for i in range(n_pre):
            x = x + _dot(pre[2 * i][rows, :], pre[2 * i + 1][...])
        xn = _rms_scale(x, g_ref[...]).astype(BF16)
        acc = jnp.zeros(x.shape, F32)
        for c in range(D_FF // FFN_CHUNK):
            cols = slice(c * FFN_CHUNK, (c + 1) * FFN_CHUNK)
            gate = _dot(xn, wg_ref[:, cols])
            up = _dot(xn, wu_ref[:, cols])
            h = (gate / (1.0 + jnp.exp(-gate)) * up).astype(BF16)
            acc = acc + _dot(h, wd_ref[cols, :])
        o_ref[rows, :] = x + 0.5 * acc


def _layer_slot(shape, layer, slot):
    return pl.BlockSpec((None, None) + shape, lambda i: (layer, slot, 0, 0), pipeline_mode=pl.Buffered(1))


def ffn(x, g, wg, wu, wd, layer, slot, pre=()):
    n = x.shape[0]
    step = FFN_SUB * FFN_TOKENS
    tile = pl.BlockSpec((step, D_MODEL), lambda i: (i, 0))
    in_specs, args = [tile], [x]
    for act, w in pre:
        in_specs += [pl.BlockSpec((step, act.shape[1]), lambda i: (i, 0)), _resident(w.shape)]
        args += [act, w]
    in_specs += [_resident((1, D_MODEL)), _layer_slot((D_MODEL, D_FF), layer, slot),
                 _layer_slot((D_MODEL, D_FF), layer, slot), _layer_slot((D_FF, D_MODEL), layer, slot)]
    args += [g, wg, wu, wd]
    return pl.pallas_call(
        functools.partial(_ffn_kernel, len(pre)),
        out_shape=jax.ShapeDtypeStruct(x.shape, F32),
        grid=(n // step,),
        in_specs=in_specs,
        out_specs=tile,
        compiler_params=_params(),
        name="ffn",
    )(*args)


def _shift_rows(x, k):
    return pltpu.roll(x, k % x.shape[0], 0)


def _window_sum(ue, w):
    p = ue + _shift_rows(ue, 1)
    if w == 2:
        return p
    span = 2
    while 2 * span < w:
        p = p + _shift_rows(p, span)
        span *= 2
    return _shift_rows(p, 1) + _shift_rows(p, -(span - 1))


def _mix0_in_kernel(x_ref, g_ref, win_ref, pw_ref, ps_ref, sw_ref, sb_ref, a_ref, z_ref, x0_ref, pad_ref):
    xn = _rms_scale(x_ref[0], g_ref[...]).astype(BF16)
    row8 = lax.broadcasted_iota(jnp.int32, (POOL_PAD, 1), 0)

    zeros = jnp.zeros((POOL_PAD, MXU_N), F32)
    pad_ref[0:POOL_PAD, :] = zeros
    pad_ref[POOL_PAD + SEQ:POOL_PAD + SEQ + POOL_PAD, :] = zeros
    for pp in range(POOL_WIDTH // MXU_N):
        u = _dot(xn, win_ref[:, pp * MXU_N:(pp + 1) * MXU_N])
        pad_ref[POOL_PAD:POOL_PAD + SEQ, :] = u
        ue = pad_ref[...]
        halves = []
        for j in range(MXU_N // LANES):
            w = POOL_WINDOWS[pp * (MXU_N // LANES) + j]
            lanes = slice(j * LANES, (j + 1) * LANES)
            win = _window_sum(ue[:, lanes], w)[POOL_PAD:POOL_PAD + SEQ]
            uj = u[:, lanes]

            def edge(r0, win=win, uj=uj, w=w):
                rr = r0 + row8
                cnt = (jnp.minimum(rr + (w - w // 2 - 1), SEQ - 1) - jnp.maximum(rr - w // 2, 0) + 1).astype(F32)
                return win[r0:r0 + POOL_PAD] / cnt - uj[r0:r0 + POOL_PAD]

            inner = win[POOL_PAD:SEQ - POOL_PAD] * (1.0 / w) - uj[POOL_PAD:SEQ - POOL_PAD]
            halves.append(jnp.concatenate([edge(0), inner, edge(SEQ - POOL_PAD)], axis=0))
        pooled = jnp.concatenate(halves, axis=1).astype(BF16)
        mixed = _dot(pooled, pw_ref[pp]) * ps_ref[:, pp * MXU_N:(pp + 1) * MXU_N]
        a_ref[0, :, pp * MXU_N:(pp + 1) * MXU_N] = mixed.astype(BF16)

    first_row = row8 == 0
    last_row = row8 == POOL_PAD - 1
    for cc in range(HYENA_WIDTH // MXU_N):
        parts = []
        for part in range(3):
            c0 = part * HYENA_WIDTH + cc * MXU_N
            p = _dot(xn, win_ref[:, POOL_WIDTH + c0:POOL_WIDTH + c0 + MXU_N])
            w = sw_ref[:, c0:c0 + MXU_N]
            bias = sb_ref[:, c0:c0 + MXU_N]
            prev = _shift_rows(p, 1)
            nxt = _shift_rows(p, -1)
            conv = prev * w[0:1] + p * w[1:2] + nxt * w[2:3] + bias
            top = slice(0, POOL_PAD)
            bot = slice(SEQ - POOL_PAD, SEQ)
            conv_top = jnp.where(first_row, 0.0, prev[top]) * w[0:1] + p[top] * w[1:2] + nxt[top] * w[2:3] + bias
            conv_bot = prev[bot] * w[0:1] + p[bot] * w[1:2] + jnp.where(last_row, 0.0, nxt[bot]) * w[2:3] + bias
            parts.append(jnp.concatenate([conv_top, conv[POOL_PAD:SEQ - POOL_PAD], conv_bot], axis=0))
        x0c, x1c, vc = parts
        zc = vc * x1c
        for j in range(MXU_N // LANES):
            z_ref[0, cc * (MXU_N // LANES) + j] = zc[:, j * LANES:(j + 1) * LANES]
        x0_ref[0, :, cc * MXU_N:(cc + 1) * MXU_N] = x0c


def mix0_in(x, g, w_in, pool_w2, pool_scale, short_w, short_b):
    b = x.shape[0]
    return pl.pallas_call(
        _mix0_in_kernel,
        out_shape=(jax.ShapeDtypeStruct((b, SEQ, POOL_WIDTH), BF16),
                   jax.ShapeDtypeStruct((b, HYENA_WIDTH // LANES, SEQ, LANES), F32),
                   jax.ShapeDtypeStruct((b, SEQ, HYENA_WIDTH), F32)),
        grid=(b,),
        in_specs=[
            pl.BlockSpec((1, SEQ, D_MODEL), lambda i: (i, 0, 0)),
            _resident((1, D_MODEL)),
            _resident(w_in.shape),
            _resident(pool_w2.shape),
            _resident((1, POOL_WIDTH)),
            _resident(short_w.shape),
            _resident(short_b.shape),
        ],
        out_specs=(pl.BlockSpec((1, SEQ, POOL_WIDTH), lambda i: (i, 0, 0)),
                   pl.BlockSpec((1, HYENA_WIDTH // LANES, SEQ, LANES), lambda i: (i, 0, 0, 0)),
                   pl.BlockSpec((1, SEQ, HYENA_WIDTH), lambda i: (i, 0, 0))),
        scratch_shapes=[pltpu.VMEM((SEQ + 2 * POOL_PAD, MXU_N), F32)],
        compiler_params=_params(),
        name="mix0_in",
    )(x, g, w_in, pool_w2, pool_scale, short_w, short_b)


def _dft_matrices():
    f = np.arange(DFT_FP, dtype=np.int64)[:, None]
    s = np.arange(DFT_H, dtype=np.int64)[None, :]
    keep = f <= DFT_H
    unit = 2.0 * np.pi / DFT_N
    ang_e = unit * ((f * (2 * s)) % DFT_N)
    ang_o = unit * ((f * (2 * s + 1)) % DFT_N)
    fwd = [np.where(keep, m, 0.0) for m in (np.cos(ang_e), np.sin(ang_e), np.cos(ang_o), np.sin(ang_o))]
    t = np.arange(DFT_H, dtype=np.int64)[:, None]
    fi = np.arange(DFT_H, dtype=np.int64)[None, :]
    wgt = np.where(fi == 0, 1.0, 2.0) / DFT_N
    ph_e = unit * ((fi * (2 * t)) % DFT_N)
    ph_o = unit * ((fi * (2 * t + 1)) % DFT_N)
    inv = [wgt * np.cos(ph_e), -wgt * np.sin(ph_e), wgt * np.cos(ph_o), -wgt * np.sin(ph_o)]
    return fwd, inv


def _filter_positions():
    m = np.arange(HY_T)
    return np.stack([m, HY_T + m, HY_T - m])


def _filter_embedding():
    pos = _filter_positions().reshape(-1)
    t_norm = np.linspace(0.0, 1.0, SEQ, dtype=np.float64)[pos]
    bands = np.linspace(1e-4, HYENA_BANDS - 1, HYENA_BANDS, dtype=np.float64)
    ang = (2.0 * math.pi / SEQ) * pos.astype(np.float64)[:, None] * bands[None, :]
    out = np.zeros((pos.shape[0], LANES), np.float64)
    out[:, :HYENA_EMB] = np.concatenate([t_norm[:, None], np.cos(ang), -np.sin(ang)], axis=-1)
    return out, t_norm[:, None]


_BLOCK_FILTER_SOURCES = (
    ((2, 1), (1, 1)),
    ((0, 0), (0, 1)),
    ((1, 0), (2, 0)),
)


def _filter_kernel(emb_ref, tn_ref, w1_ref, b1_ref, w2_ref, b2_ref, w3_ref, b3_ref, fr_ref, wo_ref, dl_ref,
                   ce_ref, se_ref, co_ref, so_ref, k1r_ref, k1i_ref, k2r_ref, k2i_ref, h_ref):
    fr = fr_ref[...]
    n_chunks = HYENA_WIDTH // LANES
    for ps in range(3):
        rows = slice(ps * HY_T, (ps + 1) * HY_T)
        h = jnp.sin(fr * (_dot(emb_ref[rows, :].astype(BF16), w1_ref[...]) + b1_ref[...]))
        h = jnp.sin(fr * (_dot(h.astype(BF16), w2_ref[...]) + b2_ref[...]))
        h = jnp.sin(fr * (_dot(h.astype(BF16), w3_ref[...]) + b3_ref[...]))
        ho = _dot(h.astype(BF16), wo_ref[...])
        decay = jnp.exp(-tn_ref[rows, :] * jnp.abs(dl_ref[...]))
        hm = ho * (decay + HYENA_MOD_SHIFT)
        for direction in range(2):
            for c in range(n_chunks):
                c0 = direction * HYENA_WIDTH + c * LANES
                h_ref[2 * ps + direction, c] = hm[:, c0:c0 + LANES]

    first = lax.broadcasted_iota(jnp.int32, (DFT_H, 1), 0) == 0

    def parity(src, par):
        ps, direction = src
        return jnp.concatenate([h_ref[2 * ps + direction, c, pl.ds(par, DFT_H, stride=2), :]
                                for c in range(n_chunks)], axis=1)

    for fi, (fwd_src, bwd_src) in enumerate(_BLOCK_FILTER_SOURCES):
        parts = []
        for par, (cos_ref, sin_ref) in enumerate(((ce_ref, se_ref), (co_ref, so_ref))):
            fwd = parity(fwd_src, par)
            bwd = parity(bwd_src, par)
            if par == 0:
                bwd = jnp.where(first, 0.0, bwd)
            parts.append((_dot(cos_ref[...], (fwd + bwd).astype(BF16)), _dot(sin_ref[...], (bwd - fwd).astype(BF16))))
        (even_re, even_im), (odd_re, odd_im) = parts
        k1r_ref[fi] = even_re + odd_re
        k1i_ref[fi] = even_im + odd_im
        k2r_ref[fi] = even_re - odd_re
        k2i_ref[fi] = even_im - odd_im


def hyena_filter(emb, tnorm, w1, b1, w2, b2, w3, b3, freq, w_out, deltas, fwd_mats):
    spec = jax.ShapeDtypeStruct((HY_FILTERS, DFT_FP, HYENA_WIDTH), F32)
    return pl.pallas_call(
        _filter_kernel,
        out_shape=(spec,) * 4,
        scratch_shapes=[pltpu.VMEM((6, HYENA_WIDTH // LANES, HY_T, LANES), F32)],
        compiler_params=pltpu.CompilerParams(vmem_limit_bytes=VMEM_LIMIT_BYTES),
        name="hyena_filter",
    )(emb, tnorm, w1, b1, w2, b2, w3, b3, freq, w_out, deltas, *fwd_mats)


def _hyena_conv_kernel(z_ref, x0_ref, hb_ref, k1r_ref, k1i_ref, k2r_ref, k2i_ref,
                       ce_ref, se_ref, co_ref, so_ref, ice_ref, ise_ref, ico_ref, iso_ref, o_ref, y_ref):
    n_chunks = MXU_N // LANES

    def samples(par):
        return jnp.concatenate([z_ref[0, c, pl.ds(j * HY_T + par, DFT_H, stride=2), :]
                                for j in range(HY_BLOCKS) for c in range(n_chunks)], axis=1).astype(BF16)

    ze = samples(0)
    zo = samples(1)
    er = _dot(ce_ref[...], ze)
    ei = _dot(se_ref[...], ze)
    orr = _dot(co_ref[...], zo)
    oi = _dot(so_ref[...], zo)
    x1r, x1i, x2r, x2i = er + orr, ei + oi, er - orr, ei - oi
    a_r, a_i, d_r, d_i = [], [], [], []
    for i in range(HY_BLOCKS):
        y1r = y1i = y2r = y2i = 0.0
        for j in range(HY_BLOCKS):
            lanes = slice(j * MXU_N, (j + 1) * MXU_N)
            fi = i - j + HY_BLOCKS - 1
            k1r, k1i, k2r, k2i = k1r_ref[fi], k1i_ref[fi], k2r_ref[fi], k2i_ref[fi]
            y1r = y1r + (x1r[:, lanes] * k1r + x1i[:, lanes] * k1i)
            y1i = y1i + (x1r[:, lanes] * k1i - x1i[:, lanes] * k1r)
            y2r = y2r + (x2r[:, lanes] * k2r + x2i[:, lanes] * k2i)
            y2i = y2i + (x2r[:, lanes] * k2i - x2i[:, lanes] * k2r)
        a_r.append(y1r + y2r)
        a_i.append(y1i + y2i)
        d_r.append(y1r - y2r)
        d_i.append(y1i - y2i)
    a_r, a_i, d_r, d_i = [jnp.concatenate(v, axis=1) for v in (a_r, a_i, d_r, d_i)]
    trow = lax.broadcasted_iota(jnp.int32, (DFT_H, 1), 0)
    sign = jnp.where((trow & 1) == 0, 1.0 / DFT_N, -1.0 / DFT_N)
    ye = (_dot(ice_ref[...], a_r[:DFT_H].astype(BF16)) + _dot(ise_ref[...], a_i[:DFT_H].astype(BF16))
          + sign * a_r[DFT_H:DFT_H + 1])
    yo = (_dot(ico_ref[...], d_r[:DFT_H].astype(BF16)) + _dot(iso_ref[...], d_i[:DFT_H].astype(BF16))
          - sign * d_i[DFT_H:DFT_H + 1])
    for i in range(HY_BLOCKS):
        for c in range(n_chunks):
            lanes = slice(i * MXU_N + c * LANES, i * MXU_N + (c + 1) * LANES)
            y_ref[c, pl.ds(i * HY_T, DFT_H, stride=2), :] = ye[:, lanes]
            y_ref[c, pl.ds(i * HY_T + 1, DFT_H, stride=2), :] = yo[:, lanes]
    for c in range(n_chunks):
        lanes = slice(c * LANES, (c + 1) * LANES)
        y = y_ref[c] + z_ref[0, c] * hb_ref[:, lanes]
        o_ref[0, :, lanes] = (y * x0_ref[0, :, lanes]).astype(BF16)


def hyena_conv(z, x0, h_bias, spectra, fwd_mats, inv_mats):
    b = z.shape[0]
    n_cc = HYENA_WIDTH // MXU_N
    spec_block = pl.BlockSpec((HY_FILTERS, DFT_FP, MXU_N), lambda c, i: (0, 0, c))
    return pl.pallas_call(
        _hyena_conv_kernel,
        out_shape=jax.ShapeDtypeStruct((b, SEQ, HYENA_WIDTH), BF16),
        grid=(n_cc, b),
        in_specs=[
            pl.BlockSpec((1, MXU_N // LANES, SEQ, LANES), lambda c, i: (i, c, 0, 0)),
            pl.BlockSpec((1, SEQ, MXU_N), lambda c, i: (i, 0, c)),
            pl.BlockSpec((1, MXU_N), lambda c, i: (0, c)),
            spec_block, spec_block, spec_block, spec_block,
        ] + [_resident(m.shape, 2) for m in fwd_mats + inv_mats],
        out_specs=pl.BlockSpec((1, SEQ, MXU_N), lambda c, i: (i, 0, c)),
        scratch_shapes=[pltpu.VMEM((MXU_N // LANES, SEQ, LANES), F32)],
        compiler_params=_params(2),
        name="hyena_conv",
    )(z, x0, h_bias, *spectra, *fwd_mats, *inv_mats)


def _head_rms(t, g):
    low = lax.broadcasted_iota(jnp.int32, (1, LANES), 1) < HEAD_DIM
    blocks = []
    for j in range(D_MODEL // LANES):
        blk = t[:, j * LANES:(j + 1) * LANES]
        sq = blk * blk
        s_lo = jnp.sum(jnp.where(low, sq, 0.0), axis=-1, keepdims=True)
        s_hi = jnp.sum(jnp.where(low, 0.0, sq), axis=-1, keepdims=True)
        ms = jnp.where(low, s_lo, s_hi) * (1.0 / HEAD_DIM)
        blocks.append(blk * lax.rsqrt(ms + EPS))
    return jnp.concatenate(blocks, axis=1) * g


def _qkv_kernel(x_ref, g_ref, w_ref, gq_ref, gk_ref, q_ref, k_ref, v_ref):
    for sub in range(FFN_SUB):
        rows = slice(sub * FFN_TOKENS, (sub + 1) * FFN_TOKENS)
        xn = _rms_scale(x_ref[rows, :], g_ref[...]).astype(BF16)
        q = _dot(xn, w_ref[:, 0:D_MODEL])
        q_ref[rows, :] = (_head_rms(q, gq_ref[...]) * ATT_Q_SCALE).astype(BF16)
        k = _dot(xn, w_ref[:, D_MODEL:2 * D_MODEL])
        k_ref[rows, :] = _head_rms(k, gk_ref[...]).astype(BF16)
        v_ref[rows, :] = _dot(xn, w_ref[:, 2 * D_MODEL:3 * D_MODEL]).astype(BF16)


def qkv_proj(x, g, w_qkv, gq, gk):
    n = x.shape[0]
    step = FFN_SUB * FFN_TOKENS
    tile = pl.BlockSpec((step, D_MODEL), lambda i: (i, 0))
    out = jax.ShapeDtypeStruct((n, D_MODEL), BF16)
    return pl.pallas_call(
        _qkv_kernel,
        out_shape=(out, out, out),
        grid=(n // step,),
        in_specs=[tile, _resident((1, D_MODEL)), _resident(w_qkv.shape), _resident((1, D_MODEL)),
                  _resident((1, D_MODEL))],
        out_specs=(tile, tile, tile),
        compiler_params=_params(),
        name="qkv_proj",
    )(x, g, w_qkv, gq, gk)


def _att_group_start(gi):
    return min(max(ATT_QROWS * gi - NA_ROWS // 2, 0), ROWS - ATT_KROWS)


def _att_group_type(gi):
    if gi < 2:
        return gi
    if gi >= ATT_GROUPS - 2:
        return ATT_TYPES - (ATT_GROUPS - gi)
    return 2


def _att_row_offsets():
    dr = np.full((ATT_TYPES, ATT_QROWS, ATT_KROWS), ATT_DR_INVALID, np.int32)
    reps = {0: 0, 1: 1, 2: 2, 3: ATT_GROUPS - 2, 4: ATT_GROUPS - 1}
    for ty, gi in reps.items():
        start = _att_group_start(gi)
        for j in range(ATT_QROWS):
            qrow = ATT_QROWS * gi + j
            s_q = min(max(qrow - NA_ROWS // 2, 0), ROWS - NA_ROWS)
            for i in range(ATT_KROWS):
                krow = start + i
                if s_q <= krow < s_q + NA_ROWS:
                    dr[ty, j, i] = krow - qrow + NA_ROWS - 1
    return dr


def _attn_kernel(q_ref, k_ref, v_ref, tab_ref, o_ref, bias_ref, vt_ref, s_ref, p_ref):
    low = lax.broadcasted_iota(jnp.int32, (1, LANES), 1) < HEAD_DIM
    dr = _att_row_offsets()

    @pl.when(pl.program_id(1) == 0)
    def _build_bias():
        for ty in range(ATT_TYPES):
            for i in range(ATT_KROWS):
                for hh in range(2):
                    tile = jnp.where(low, tab_ref[hh, int(dr[ty, 0, i])], tab_ref[hh, int(dr[ty, 1, i])])
                    bias_ref[ty, i * GRID_W:(i + 1) * GRID_W, hh * LANES:(hh + 1) * LANES] = tile

    vt_ref[0:LANES, :] = v_ref[0].T
    vt_ref[LANES:LANES + ATT_ONES, :] = jnp.ones((ATT_ONES, SEQ), BF16)

    def scores(gi):
        q0 = gi * ATT_Q
        k0 = _att_group_start(gi) * GRID_W
        qg = q_ref[0, q0:q0 + ATT_Q, :]
        zero = jnp.zeros_like(qg)
        qs = jnp.concatenate([jnp.where(low, qg, zero), jnp.where(low, zero, qg)], axis=0)
        st = lax.dot_general(k_ref[0, k0:k0 + ATT_K, :], qs, (((1,), (1,)), ((), ())),
                             preferred_element_type=F32)
        s_ref[gi] = st + bias_ref[_att_group_type(gi)]

    def softmax(gi):
        s = s_ref[gi]
        m = jnp.max(s, axis=0, keepdims=True)
        p_ref[gi] = jnp.exp2(s - m).astype(BF16)

    def values(gi):
        q0 = gi * ATT_Q
        k0 = _att_group_start(gi) * GRID_W
        r = _dot(vt_ref[:, k0:k0 + ATT_K], p_ref[gi])
        den = r[LANES:LANES + 1]
        o0 = r[0:HEAD_DIM, 0:LANES] / den[:, 0:LANES]
        o1 = r[HEAD_DIM:LANES, LANES:2 * LANES] / den[:, LANES:2 * LANES]
        o_ref[0, q0:q0 + ATT_Q, :] = jnp.concatenate([o0, o1], axis=0).T.astype(BF16)

    for step in range(ATT_GROUPS + 2):
        if step < ATT_GROUPS:
            scores(step)
        if 1 <= step <= ATT_GROUPS:
            softmax(step - 1)
        if step >= 2:
            values(step - 2)


def attention(q, k, v, table):
    b = q.shape[0]
    blk = pl.BlockSpec((1, SEQ, LANES), lambda h, i: (i, 0, h))
    return pl.pallas_call(
        _attn_kernel,
        out_shape=jax.ShapeDtypeStruct((b, SEQ, D_MODEL), BF16),
        grid=(N_HEADS // 2, b),
        in_specs=[blk, blk, blk,
                  pl.BlockSpec((2, ATT_DR_INVALID + 1, GRID_W, LANES), lambda h, i: (h, 0, 0, 0))],
        out_specs=blk,
        scratch_shapes=[pltpu.VMEM((ATT_TYPES, ATT_K, 2 * ATT_Q), F32),
                        pltpu.VMEM((LANES + ATT_ONES, SEQ), BF16),
                        pltpu.VMEM((ATT_GROUPS, ATT_K, 2 * ATT_Q), F32),
                        pltpu.VMEM((ATT_GROUPS, ATT_K, 2 * ATT_Q), BF16)],
        compiler_params=pltpu.CompilerParams(dimension_semantics=("arbitrary", "arbitrary"),
                                             vmem_limit_bytes=VMEM_LIMIT_BYTES),
        name="attention",
    )(q, k, v, table)


def _attn_bias_table(rpb):
    c = np.arange(GRID_W)
    c_start = np.clip(c - NA_COLS // 2, 0, GRID_W - NA_COLS)
    col_ok = (c[None, :] >= c_start[:, None]) & (c[None, :] < c_start[:, None] + NA_COLS)
    dc = np.clip(c[None, :] - c[:, None] + NA_COLS - 1, 0, 2 * NA_COLS - 2)
    onehot = (dc.T[None] == np.arange(2 * NA_COLS - 1)[:, None, None]).astype(np.float32)
    tab = jnp.einsum("hrd,dkq->hrkq", rpb.astype(F32), jnp.asarray(onehot), precision=lax.Precision.HIGHEST)
    tab = jnp.where(jnp.asarray(col_ok.T)[None, None], tab * LOG2_E, NEG_BIG)
    tab = jnp.concatenate([tab, jnp.full((N_HEADS, 1, GRID_W, GRID_W), NEG_BIG, F32)], axis=1)
    return jnp.concatenate([tab, tab], axis=-1)


def _block_diag_pairs(pool_w):
    z = jnp.zeros((LANES, LANES), pool_w.dtype)
    pairs = []
    for pp in range(2):
        top = jnp.concatenate([pool_w[2 * pp], z], axis=1)
        bot = jnp.concatenate([z, pool_w[2 * pp + 1]], axis=1)
        pairs.append(jnp.concatenate([top, bot], axis=0))
    return jnp.stack(pairs)


def kernel(x_prompt, x_sample, norm_g, ffn_w_gate, ffn_w_up, ffn_w_down, w_in_ab, pool_w, pool_scale, short_w, short_b, filt_w1, filt_b1, filt_w2, filt_b2, filt_w3, filt_b3, filt_freq, filt_w_out, filt_deltas, hyena_bias, w_out_ab, w_qkv, q_norm_g, k_norm_g, rpb, w_o):
    wg = ffn_w_gate.astype(BF16)
    wu = ffn_w_up.astype(BF16)
    wd = ffn_w_down.astype(BF16)

    def run_ffn(t, layer, slot, pre=()):
        return ffn(t, norm_g[layer, 2 * slot][None], wg, wu, wd, layer, slot, pre)

    w_in = w_in_ab[0].astype(BF16)
    pool_w2 = _block_diag_pairs(pool_w[0]).astype(BF16)
    w_qkv_b = w_qkv[0].astype(BF16)
    w_o_b = w_o[0].astype(BF16)
    gq = jnp.tile(q_norm_g[0], N_HEADS)[None]
    gk = jnp.tile(k_norm_g[0], N_HEADS)[None]

    fwd_np, inv_np = _dft_matrices()
    fwd_mats = [jnp.asarray(m).astype(BF16) for m in fwd_np]
    inv_mats = [jnp.asarray(m).astype(BF16) for m in inv_np]
    emb_np, tnorm_np = _filter_embedding()
    w1 = jnp.zeros((LANES, HYENA_HIDDEN), BF16).at[:HYENA_EMB].set(filt_w1[0].astype(BF16))
    spectra = hyena_filter(jnp.asarray(emb_np).astype(F32), jnp.asarray(tnorm_np).astype(F32), w1, filt_b1[0][None],
                           filt_w2[0].astype(BF16), filt_b2[0][None], filt_w3[0].astype(BF16), filt_b3[0][None],
                           filt_freq[0][None], filt_w_out[0].astype(BF16), filt_deltas[0][None], fwd_mats)
    table = _attn_bias_table(rpb[0])
    w_out_a = w_out_ab[0, :POOL_WIDTH].astype(BF16)
    w_out_h = w_out_ab[0, POOL_WIDTH:].astype(BF16)

    outs = []
    for x in (x_prompt, x_sample):
        b = x.shape[0]
        n = b * SEQ
        t = run_ffn(x.reshape(n, D_MODEL), 0, 0)
        a, z, x0 = mix0_in(t.reshape(b, SEQ, D_MODEL), norm_g[0, 1][None], w_in, pool_w2, pool_scale[0][None],
                           short_w[0], short_b[0][None])
        yh = hyena_conv(z, x0, hyena_bias[0][None], spectra, fwd_mats, inv_mats)
        t = run_ffn(t, 0, 1, ((a.reshape(n, POOL_WIDTH), w_out_a), (yh.reshape(n, HYENA_WIDTH), w_out_h)))
        t = run_ffn(t, 1, 0)
        q, k, v = qkv_proj(t, norm_g[1, 1][None], w_qkv_b, gq, gk)
        o = attention(q.reshape(b, SEQ, D_MODEL), k.reshape(b, SEQ, D_MODEL), v.reshape(b, SEQ, D_MODEL), table)
        t = run_ffn(t, 1, 1, ((o.reshape(n, D_MODEL), w_o_b),))
        outs.append(t.reshape(b, SEQ, D_MODEL))
    return tuple(outs)
```

```python
import functools
import math

import jax
import jax.numpy as jnp
import numpy as np
from jax import lax
from jax.experimental import pallas as pl
from jax.experimental.pallas import tpu as pltpu

BF16 = jnp.bfloat16
F32 = jnp.float32

D_MODEL = 1024
SEQ = 2048
D_FF = 2816
EPS = 1e-6
GRID_W = 64
POOL_WIDTH = 512
POOL_WINDOWS = (2, 4, 8, 16)
HYENA_WIDTH = 512
HYENA_BANDS = 16
HYENA_EMB = 1 + 2 * HYENA_BANDS
HYENA_HIDDEN = 64
HYENA_MOD_SHIFT = 0.05
N_HEADS = 16
HEAD_DIM = 64
NA_ROWS = 8
NA_COLS = 16
ROWS = SEQ // GRID_W

LANES = 128
MXU_N = 256
VMEM_LIMIT_BYTES = 56 * 1024 * 1024

FFN_TOKENS = 512
FFN_SUB = 2
FFN_CHUNK = MXU_N

HY_BLOCKS = 2
HY_T = SEQ // HY_BLOCKS
HY_FILTERS = 2 * HY_BLOCKS - 1
DFT_N = 2 * HY_T
DFT_H = HY_T // 2
DFT_FP = DFT_H + 8
POOL_PAD = 8

ATT_QROWS = 2
ATT_KROWS = 10
ATT_Q = ATT_QROWS * GRID_W
ATT_K = ATT_KROWS * GRID_W
ATT_GROUPS = ROWS // ATT_QROWS
ATT_TYPES = 5
ATT_DR_INVALID = 2 * NA_ROWS - 1
ATT_ONES = 16
ATT_STAGE_LAG = 2
NEG_BIG = -1e30
LOG2_E = math.log2(math.e)
ATT_Q_SCALE = HEAD_DIM ** -0.5 * LOG2_E


def _params(n_axes=1):
    return pltpu.CompilerParams(dimension_semantics=("parallel",) * n_axes,
                                vmem_limit_bytes=VMEM_LIMIT_BYTES)


def _resident(shape, n_axes=1):
    zeros = (0,) * len(shape)
    if n_axes == 1:
        return pl.BlockSpec(shape, lambda i: zeros, pipeline_mode=pl.Buffered(1))
    return pl.BlockSpec(shape, lambda i, j: zeros, pipeline_mode=pl.Buffered(1))


def _dot(a, b):
    return jnp.dot(a, b, preferred_element_type=F32)


def _rms_scale(x, g):
    ms = jnp.mean(x * x, axis=-1, keepdims=True)
    return x * lax.rsqrt(ms + EPS) * g


def _ffn_kernel(n_pre, x_ref, *refs):
    pre = refs[:2 * n_pre]
    g_ref, wg_ref, wu_ref, wd_ref, o_ref = refs[2 * n_pre:]
    for sub in range(FFN_SUB):
        rows = slice(sub * FFN_TOKENS, (sub + 1) * FFN_TOKENS)
        x = x_ref[rows, :]
        for i in range(n_pre):
            x = x + _dot(pre[2 * i][rows, :], pre[2 * i + 1][...])
        xn = _rms_scale(x, g_ref[...]).astype(BF16)
        acc = jnp.zeros(x.shape, F32)
        for c in range(D_FF // FFN_CHUNK):
            cols = slice(c * FFN_CHUNK, (c + 1) * FFN_CHUNK)
            gate = _dot(xn, wg_ref[:, cols])
            up = _dot(xn, wu_ref[:, cols])
            h = (gate / (1.0 + jnp.exp(-gate)) * up).astype(BF16)
            acc = acc + _dot(h, wd_ref[cols, :])
        o_ref[rows, :] = x + 0.5 * acc


def _layer_slot(shape, layer, slot):
    return pl.BlockSpec((None, None) + shape, lambda i: (layer, slot, 0, 0), pipeline_mode=pl.Buffered(1))


def ffn(x, g, wg, wu, wd, layer, slot, pre=()):
    n = x.shape[0]
    step = FFN_SUB * FFN_TOKENS
    tile = pl.BlockSpec((step, D_MODEL), lambda i: (i, 0))
    in_specs, args = [tile], [x]
    for act, w in pre:
        in_specs += [pl.BlockSpec((step, act.shape[1]), lambda i: (i, 0)), _resident(w.shape)]
        args += [act, w]
    in_specs += [_resident((1, D_MODEL)), _layer_slot((D_MODEL, D_FF), layer, slot),
                 _layer_slot((D_MODEL, D_FF), layer, slot), _layer_slot((D_FF, D_MODEL), layer, slot)]
    args += [g, wg, wu, wd]
    return pl.pallas_call(
        functools.partial(_ffn_kernel, len(pre)),
        out_shape=jax.ShapeDtypeStruct(x.shape, F32),
        grid=(n // step,),
        in_specs=in_specs,
        out_specs=tile,
        compiler_params=_params(),
        name="ffn",
    )(*args)


def _shift_rows(x, k):
    return pltpu.roll(x, k % x.shape[0], 0)


def _window_sum(ue, w):
    p = ue + _shift_rows(ue, 1)
    if w == 2:
        return p
    span = 2
    while 2 * span < w:
        p = p + _shift_rows(p, span)
        span *= 2
    return _shift_rows(p, 1) + _shift_rows(p, -(span - 1))


MIX0_SLOTS = 3


def _mix0_in_kernel(x_ref, g_ref, win_ref, pw_ref, ps_ref, sw_ref, sb_ref, a_ref, z_ref, x0_ref, p_ref, x1_ref):
    xn = _rms_scale(x_ref[0], g_ref[...]).astype(BF16)
    row8 = lax.broadcasted_iota(jnp.int32, (POOL_PAD, 1), 0)
    first_row = row8 == 0
    last_row = row8 == POOL_PAD - 1
    n_pool = POOL_WIDTH // MXU_N
    n_slices = (POOL_WIDTH + 3 * HYENA_WIDTH) // MXU_N
    body = slice(POOL_PAD, POOL_PAD + SEQ)

    zeros = jnp.zeros((POOL_PAD, MXU_N), F32)
    for slot in range(MIX0_SLOTS):
        p_ref[slot, 0:POOL_PAD, :] = zeros
        p_ref[slot, POOL_PAD + SEQ:POOL_PAD + SEQ + POOL_PAD, :] = zeros

    def project(k):
        p_ref[k % MIX0_SLOTS, body, :] = _dot(xn, win_ref[:, k * MXU_N:(k + 1) * MXU_N])

    def pool(k):
        ue = p_ref[k % MIX0_SLOTS]
        halves = []
        for j in range(MXU_N // LANES):
            w = POOL_WINDOWS[k * (MXU_N // LANES) + j]
            lanes = slice(j * LANES, (j + 1) * LANES)
            win = _window_sum(ue[:, lanes], w)[body]
            uj = ue[body, lanes]

            def edge(r0, win=win, uj=uj, w=w):
                rr = r0 + row8
                cnt = (jnp.minimum(rr + (w - w // 2 - 1), SEQ - 1) - jnp.maximum(rr - w // 2, 0) + 1).astype(F32)
                return win[r0:r0 + POOL_PAD] / cnt - uj[r0:r0 + POOL_PAD]

            inner = win[POOL_PAD:SEQ - POOL_PAD] * (1.0 / w) - uj[POOL_PAD:SEQ - POOL_PAD]
            halves.append(jnp.concatenate([edge(0), inner, edge(SEQ - POOL_PAD)], axis=0))
        pooled = jnp.concatenate(halves, axis=1).astype(BF16)
        mixed = _dot(pooled, pw_ref[k]) * ps_ref[:, k * MXU_N:(k + 1) * MXU_N]
        a_ref[0, :, k * MXU_N:(k + 1) * MXU_N] = mixed.astype(BF16)

    def short_conv(k):
        c0 = (k - n_pool) * MXU_N
        p = p_ref[k % MIX0_SLOTS, body, :]
        w = sw_ref[:, c0:c0 + MXU_N]
        bias = sb_ref[:, c0:c0 + MXU_N]
        prev = _shift_rows(p, 1)
        nxt = _shift_rows(p, -1)
        conv = prev * w[0:1] + p * w[1:2] + nxt * w[2:3] + bias
        top = slice(0, POOL_PAD)
        bot = slice(SEQ - POOL_PAD, SEQ)
        conv_top = jnp.where(first_row, 0.0, prev[top]) * w[0:1] + p[top] * w[1:2] + nxt[top] * w[2:3] + bias
        conv_bot = prev[bot] * w[0:1] + p[bot] * w[1:2] + jnp.where(last_row, 0.0, nxt[bot]) * w[2:3] + bias
        return jnp.concatenate([conv_top, conv[POOL_PAD:SEQ - POOL_PAD], conv_bot], axis=0)

    def finish(k):
        if k < n_pool:
            pool(k)
            return
        part, cc = divmod(k - n_pool, HYENA_WIDTH // MXU_N)
        conv = short_conv(k)
        if part == 0:
            x0_ref[0, :, cc * MXU_N:(cc + 1) * MXU_N] = conv
        elif part == 1:
            x1_ref[cc] = conv
        else:
            zc = conv * x1_ref[cc]
            for j in range(MXU_N // LANES):
                z_ref[0, cc * (MXU_N // LANES) + j] = zc[:, j * LANES:(j + 1) * LANES]

    project(0)
    for k in range(1, n_slices):
        project(k)
        finish(k - 1)
    finish(n_slices - 1)


def mix0_in(x, g, w_in, pool_w2, pool_scale, short_w, short_b):
    b = x.shape[0]
    return pl.pallas_call(
        _mix0_in_kernel,
        out_shape=(jax.ShapeDtypeStruct((b, SEQ, POOL_WIDTH), BF16),
                   jax.ShapeDtypeStruct((b, HYENA_WIDTH // LANES, SEQ, LANES), F32),
                   jax.ShapeDtypeStruct((b, SEQ, HYENA_WIDTH), F32)),
        grid=(b,),
        in_specs=[
            pl.BlockSpec((1, SEQ, D_MODEL), lambda i: (i, 0, 0)),
            _resident((1, D_MODEL)),
            _resident(w_in.shape),
            _resident(pool_w2.shape),
            _resident((1, POOL_WIDTH)),
            _resident(short_w.shape),
            _resident(short_b.shape),
        ],
        out_specs=(pl.BlockSpec((1, SEQ, POOL_WIDTH), lambda i: (i, 0, 0)),
                   pl.BlockSpec((1, HYENA_WIDTH // LANES, SEQ, LANES), lambda i: (i, 0, 0, 0)),
                   pl.BlockSpec((1, SEQ, HYENA_WIDTH), lambda i: (i, 0, 0))),
        scratch_shapes=[pltpu.VMEM((MIX0_SLOTS, SEQ + 2 * POOL_PAD, MXU_N), F32),
                        pltpu.VMEM((HYENA_WIDTH // MXU_N, SEQ, MXU_N), F32)],
        compiler_params=_params(),
        name="mix0_in",
    )(x, g, w_in, pool_w2, pool_scale, short_w, short_b)


def _dft_matrices():
    f = np.arange(DFT_FP, dtype=np.int64)[:, None]
    s = np.arange(DFT_H, dtype=np.int64)[None, :]
    keep = f <= DFT_H
    unit = 2.0 * np.pi / DFT_N
    ang_e = unit * ((f * (2 * s)) % DFT_N)
    ang_o = unit * ((f * (2 * s + 1)) % DFT_N)
    fwd = [np.where(keep, m, 0.0) for m in (np.cos(ang_e), np.sin(ang_e), np.cos(ang_o), np.sin(ang_o))]
    t = np.arange(DFT_H, dtype=np.int64)[:, None]
    fi = np.arange(DFT_H, dtype=np.int64)[None, :]
    wgt = np.where(fi == 0, 1.0, 2.0) / DFT_N
    ph_e = unit * ((fi * (2 * t)) % DFT_N)
    ph_o = unit * ((fi * (2 * t + 1)) % DFT_N)
    inv = [wgt * np.cos(ph_e), -wgt * np.sin(ph_e), wgt * np.cos(ph_o), -wgt * np.sin(ph_o)]
    return fwd, inv


def _filter_positions():
    m = np.arange(HY_T)
    return np.stack([m, HY_T + m, HY_T - m])


def _filter_embedding():
    pos = _filter_positions().reshape(-1)
    t_norm = np.linspace(0.0, 1.0, SEQ, dtype=np.float64)[pos]
    bands = np.linspace(1e-4, HYENA_BANDS - 1, HYENA_BANDS, dtype=np.float64)
    ang = (2.0 * math.pi / SEQ) * pos.astype(np.float64)[:, None] * bands[None, :]
    out = np.zeros((pos.shape[0], LANES), np.float64)
    out[:, :HYENA_EMB] = np.concatenate([t_norm[:, None], np.cos(ang), -np.sin(ang)], axis=-1)
    return out, t_norm[:, None]


_BLOCK_FILTER_SOURCES = (
    ((2, 1), (1, 1)),
    ((0, 0), (0, 1)),
    ((1, 0), (2, 0)),
)


def _filter_kernel(emb_ref, tn_ref, w1_ref, b1_ref, w2_ref, b2_ref, w3_ref, b3_ref, fr_ref, wo_ref, dl_ref,
                   ce_ref, se_ref, co_ref, so_ref, k1r_ref, k1i_ref, k2r_ref, k2i_ref, h_ref):
    fr = fr_ref[...]
    n_chunks = HYENA_WIDTH // LANES
    for ps in range(3):
        rows = slice(ps * HY_T, (ps + 1) * HY_T)
        h = jnp.sin(fr * (_dot(emb_ref[rows, :].astype(BF16), w1_ref[...]) + b1_ref[...]))
        h = jnp.sin(fr * (_dot(h.astype(BF16), w2_ref[...]) + b2_ref[...]))
        h = jnp.sin(fr * (_dot(h.astype(BF16), w3_ref[...]) + b3_ref[...]))
        ho = _dot(h.astype(BF16), wo_ref[...])
        decay = jnp.exp(-tn_ref[rows, :] * jnp.abs(dl_ref[...]))
        hm = ho * (decay + HYENA_MOD_SHIFT)
        for direction in range(2):
            for c in range(n_chunks):
                c0 = direction * HYENA_WIDTH + c * LANES
                h_ref[2 * ps + direction, c] = hm[:, c0:c0 + LANES]

    first = lax.broadcasted_iota(jnp.int32, (DFT_H, 1), 0) == 0

    def parity(src, par):
        ps, direction = src
        return jnp.concatenate([h_ref[2 * ps + direction, c, pl.ds(par, DFT_H, stride=2), :]
                                for c in range(n_chunks)], axis=1)

    for fi, (fwd_src, bwd_src) in enumerate(_BLOCK_FILTER_SOURCES):
        parts = []
        for par, (cos_ref, sin_ref) in enumerate(((ce_ref, se_ref), (co_ref, so_ref))):
            fwd = parity(fwd_src, par)
            bwd = parity(bwd_src, par)
            if par == 0:
                bwd = jnp.where(first, 0.0, bwd)
            parts.append((_dot(cos_ref[...], (fwd + bwd).astype(BF16)), _dot(sin_ref[...], (bwd - fwd).astype(BF16))))
        (even_re, even_im), (odd_re, odd_im) = parts
        k1r_ref[fi] = even_re + odd_re
        k1i_ref[fi] = even_im + odd_im
        k2r_ref[fi] = even_re - odd_re
        k2i_ref[fi] = even_im - odd_im


def hyena_filter(emb, tnorm, w1, b1, w2, b2, w3, b3, freq, w_out, deltas, fwd_mats):
    spec = jax.ShapeDtypeStruct((HY_FILTERS, DFT_FP, HYENA_WIDTH), F32)
    return pl.pallas_call(
        _filter_kernel,
        out_shape=(spec,) * 4,
        scratch_shapes=[pltpu.VMEM((6, HYENA_WIDTH // LANES, HY_T, LANES), F32)],
        compiler_params=pltpu.CompilerParams(vmem_limit_bytes=VMEM_LIMIT_BYTES),
        name="hyena_filter",
    )(emb, tnorm, w1, b1, w2, b2, w3, b3, freq, w_out, deltas, *fwd_mats)


def _hyena_conv_kernel(z_ref, x0_ref, hb_ref, k1r_ref, k1i_ref, k2r_ref, k2i_ref,
                       ce_ref, se_ref, co_ref, so_ref, ice_ref, ise_ref, ico_ref, iso_ref, o_ref, y_ref):
    n_chunks = MXU_N // LANES

    def samples(par):
        return jnp.concatenate([z_ref[0, c, pl.ds(j * HY_T + par, DFT_H, stride=2), :]
                                for j in range(HY_BLOCKS) for c in range(n_chunks)], axis=1).astype(BF16)

    ze = samples(0)
    zo = samples(1)
    er = _dot(ce_ref[...], ze)
    ei = _dot(se_ref[...], ze)
    orr = _dot(co_ref[...], zo)
    oi = _dot(so_ref[...], zo)
    x1r, x1i, x2r, x2i = er + orr, ei + oi, er - orr, ei - oi
    a_r, a_i, d_r, d_i = [], [], [], []
    for i in range(HY_BLOCKS):
        y1r = y1i = y2r = y2i = 0.0
        for j in range(HY_BLOCKS):
            lanes = slice(j * MXU_N, (j + 1) * MXU_N)
            fi = i - j + HY_BLOCKS - 1
            k1r, k1i, k2r, k2i = k1r_ref[fi], k1i_ref[fi], k2r_ref[fi], k2i_ref[fi]
            y1r = y1r + (x1r[:, lanes] * k1r + x1i[:, lanes] * k1i)
            y1i = y1i + (x1r[:, lanes] * k1i - x1i[:, lanes] * k1r)
            y2r = y2r + (x2r[:, lanes] * k2r + x2i[:, lanes] * k2i)
            y2i = y2i + (x2r[:, lanes] * k2i - x2i[:, lanes] * k2r)
        a_r.append(y1r + y2r)
        a_i.append(y1i + y2i)
        d_r.append(y1r - y2r)
        d_i.append(y1i - y2i)
    a_r, a_i, d_r, d_i = [jnp.concatenate(v, axis=1) for v in (a_r, a_i, d_r, d_i)]
    trow = lax.broadcasted_iota(jnp.int32, (DFT_H, 1), 0)
    sign = jnp.where((trow & 1) == 0, 1.0 / DFT_N, -1.0 / DFT_N)
    ye = (_dot(ice_ref[...], a_r[:DFT_H].astype(BF16)) + _dot(ise_ref[...], a_i[:DFT_H].astype(BF16))
          + sign * a_r[DFT_H:DFT_H + 1])
    yo = (_dot(ico_ref[...], d_r[:DFT_H].astype(BF16)) + _dot(iso_ref[...], d_i[:DFT_H].astype(BF16))
          - sign * d_i[DFT_H:DFT_H + 1])
    for i in range(HY_BLOCKS):
        for c in range(n_chunks):
            lanes = slice(i * MXU_N + c * LANES, i * MXU_N + (c + 1) * LANES)
            y_ref[c, pl.ds(i * HY_T, DFT_H, stride=2), :] = ye[:, lanes]
            y_ref[c, pl.ds(i * HY_T + 1, DFT_H, stride=2), :] = yo[:, lanes]
    for c in range(n_chunks):
        lanes = slice(c * LANES, (c + 1) * LANES)
        y = y_ref[c] + z_ref[0, c] * hb_ref[:, lanes]
        o_ref[0, :, lanes] = (y * x0_ref[0, :, lanes]).astype(BF16)


def hyena_conv(z, x0, h_bias, spectra, fwd_mats, inv_mats):
    b = z.shape[0]
    n_cc = HYENA_WIDTH // MXU_N
    spec_block = pl.BlockSpec((HY_FILTERS, DFT_FP, MXU_N), lambda c, i: (0, 0, c))
    return pl.pallas_call(
        _hyena_conv_kernel,
        out_shape=jax.ShapeDtypeStruct((b, SEQ, HYENA_WIDTH), BF16),
        grid=(n_cc, b),
        in_specs=[
            pl.BlockSpec((1, MXU_N // LANES, SEQ, LANES), lambda c, i: (i, c, 0, 0)),
            pl.BlockSpec((1, SEQ, MXU_N), lambda c, i: (i, 0, c)),
            pl.BlockSpec((1, MXU_N), lambda c, i: (0, c)),
            spec_block, spec_block, spec_block, spec_block,
        ] + [_resident(m.shape, 2) for m in fwd_mats + inv_mats],
        out_specs=pl.BlockSpec((1, SEQ, MXU_N), lambda c, i: (i, 0, c)),
        scratch_shapes=[pltpu.VMEM((MXU_N // LANES, SEQ, LANES), F32)],
        compiler_params=_params(2),
        name="hyena_conv",
    )(z, x0, h_bias, *spectra, *fwd_mats, *inv_mats)


def _head_rms(t, g):
    low = lax.broadcasted_iota(jnp.int32, (1, LANES), 1) < HEAD_DIM
    blocks = []
    for j in range(D_MODEL // LANES):
        blk = t[:, j * LANES:(j + 1) * LANES]
        sq = blk * blk
        s_lo = jnp.sum(jnp.where(low, sq, 0.0), axis=-1, keepdims=True)
        s_hi = jnp.sum(jnp.where(low, 0.0, sq), axis=-1, keepdims=True)
        ms = jnp.where(low, s_lo, s_hi) * (1.0 / HEAD_DIM)
        blocks.append(blk * lax.rsqrt(ms + EPS))
    return jnp.concatenate(blocks, axis=1) * g


def _qkv_kernel(x_ref, g_ref, w_ref, gq_ref, gk_ref, q_ref, k_ref, v_ref):
    for sub in range(FFN_SUB):
        rows = slice(sub * FFN_TOKENS, (sub + 1) * FFN_TOKENS)
        xn = _rms_scale(x_ref[rows, :], g_ref[...]).astype(BF16)
        q = _dot(xn, w_ref[:, 0:D_MODEL])
        q_ref[rows, :] = (_head_rms(q, gq_ref[...]) * ATT_Q_SCALE).astype(BF16)
        k = _dot(xn, w_ref[:, D_MODEL:2 * D_MODEL])
        k_ref[rows, :] = _head_rms(k, gk_ref[...]).astype(BF16)
        v_ref[rows, :] = _dot(xn, w_ref[:, 2 * D_MODEL:3 * D_MODEL]).astype(BF16)


def qkv_proj(x, g, w_qkv, gq, gk):
    n = x.shape[0]
    step = FFN_SUB * FFN_TOKENS
    tile = pl.BlockSpec((step, D_MODEL), lambda i: (i, 0))
    out = jax.ShapeDtypeStruct((n, D_MODEL), BF16)
    return pl.pallas_call(
        _qkv_kernel,
        out_shape=(out, out, out),
        grid=(n // step,),
        in_specs=[tile, _resident((1, D_MODEL)), _resident(w_qkv.shape), _resident((1, D_MODEL)),
                  _resident((1, D_MODEL))],
        out_specs=(tile, tile, tile),
        compiler_params=_params(),
        name="qkv_proj",
    )(x, g, w_qkv, gq, gk)


def _att_group_start(gi):
    return min(max(ATT_QROWS * gi - NA_ROWS // 2, 0), ROWS - ATT_KROWS)


def _att_group_type(gi):
    if gi < 2:
        return gi
    if gi >= ATT_GROUPS - 2:
        return ATT_TYPES - (ATT_GROUPS - gi)
    return 2


def _att_row_offsets():
    dr = np.full((ATT_TYPES, ATT_QROWS, ATT_KROWS), ATT_DR_INVALID, np.int32)
    reps = {0: 0, 1: 1, 2: 2, 3: ATT_GROUPS - 2, 4: ATT_GROUPS - 1}
    for ty, gi in reps.items():
        start = _att_group_start(gi)
        for j in range(ATT_QROWS):
            qrow = ATT_QROWS * gi + j
            s_q = min(max(qrow - NA_ROWS // 2, 0), ROWS - NA_ROWS)
            for i in range(ATT_KROWS):
                krow = start + i
                if s_q <= krow < s_q + NA_ROWS:
                    dr[ty, j, i] = krow - qrow + NA_ROWS - 1
    return dr


def _attn_kernel(q_ref, k_ref, v_ref, tab_ref, o_ref, bias_ref, vt_ref, s_ref, p_ref):
    low = lax.broadcasted_iota(jnp.int32, (1, LANES), 1) < HEAD_DIM
    dr = _att_row_offsets()

    @pl.when(pl.program_id(1) == 0)
    def _build_bias():
        for ty in range(ATT_TYPES):
            for i in range(ATT_KROWS):
                for hh in range(2):
                    tile = jnp.where(low, tab_ref[hh, int(dr[ty, 0, i])], tab_ref[hh, int(dr[ty, 1, i])])
                    bias_ref[ty, i * GRID_W:(i + 1) * GRID_W, hh * LANES:(hh + 1) * LANES] = tile

    vt_ref[0:LANES, :] = v_ref[0].T
    vt_ref[LANES:LANES + ATT_ONES, :] = jnp.ones((ATT_ONES, SEQ), BF16)

    def scores(gi):
        q0 = gi * ATT_Q
        k0 = _att_group_start(gi) * GRID_W
        qg = q_ref[0, q0:q0 + ATT_Q, :]
        zero = jnp.zeros_like(qg)
        qs = jnp.concatenate([jnp.where(low, qg, zero), jnp.where(low, zero, qg)], axis=0)
        st = lax.dot_general(k_ref[0, k0:k0 + ATT_K, :], qs, (((1,), (1,)), ((), ())),
                             preferred_element_type=F32)
        s_ref[gi] = st + bias_ref[_att_group_type(gi)]

    def softmax(gi):
        s = s_ref[gi]
        m = jnp.max(s, axis=0, keepdims=True)
        p_ref[gi] = jnp.exp2(s - m).astype(BF16)

    def values(gi):
        q0 = gi * ATT_Q
        k0 = _att_group_start(gi) * GRID_W
        r = _dot(vt_ref[:, k0:k0 + ATT_K], p_ref[gi])
        den = r[LANES:LANES + 1]
        o0 = r[0:HEAD_DIM, 0:LANES] / den[:, 0:LANES]
        o1 = r[HEAD_DIM:LANES, LANES:2 * LANES] / den[:, LANES:2 * LANES]
        o_ref[0, q0:q0 + ATT_Q, :] = jnp.concatenate([o0, o1], axis=0).T.astype(BF16)

    for step in range(ATT_GROUPS + 2 * ATT_STAGE_LAG):
        if step < ATT_GROUPS:
            scores(step)
        if 0 <= step - ATT_STAGE_LAG < ATT_GROUPS:
            softmax(step - ATT_STAGE_LAG)
        if 0 <= step - 2 * ATT_STAGE_LAG < ATT_GROUPS:
            values(step - 2 * ATT_STAGE_LAG)


def attention(q, k, v, table):
    b = q.shape[0]
    blk = pl.BlockSpec((1, SEQ, LANES), lambda h, i: (i, 0, h))
    return pl.pallas_call(
        _attn_kernel,
        out_shape=jax.ShapeDtypeStruct((b, SEQ, D_MODEL), BF16),
        grid=(N_HEADS // 2, b),
        in_specs=[blk, blk, blk,
                  pl.BlockSpec((2, ATT_DR_INVALID + 1, GRID_W, LANES), lambda h, i: (h, 0, 0, 0))],
        out_specs=blk,
        scratch_shapes=[pltpu.VMEM((ATT_TYPES, ATT_K, 2 * ATT_Q), F32),
                        pltpu.VMEM((LANES + ATT_ONES, SEQ), BF16),
                        pltpu.VMEM((ATT_GROUPS, ATT_K, 2 * ATT_Q), F32),
                        pltpu.VMEM((ATT_GROUPS, ATT_K, 2 * ATT_Q), BF16)],
        compiler_params=pltpu.CompilerParams(dimension_semantics=("arbitrary", "arbitrary"),
                                             vmem_limit_bytes=VMEM_LIMIT_BYTES),
        name="attention",
    )(q, k, v, table)


def _attn_bias_table(rpb):
    c = np.arange(GRID_W)
    c_start = np.clip(c - NA_COLS // 2, 0, GRID_W - NA_COLS)
    col_ok = (c[None, :] >= c_start[:, None]) & (c[None, :] < c_start[:, None] + NA_COLS)
    dc = np.clip(c[None, :] - c[:, None] + NA_COLS - 1, 0, 2 * NA_COLS - 2)
    onehot = (dc.T[None] == np.arange(2 * NA_COLS - 1)[:, None, None]).astype(np.float32)
    tab = jnp.einsum("hrd,dkq->hrkq", rpb.astype(F32), jnp.asarray(onehot), precision=lax.Precision.HIGHEST)
    tab = jnp.where(jnp.asarray(col_ok.T)[None, None], tab * LOG2_E, NEG_BIG)
    tab = jnp.concatenate([tab, jnp.full((N_HEADS, 1, GRID_W, GRID_W), NEG_BIG, F32)], axis=1)
    return jnp.concatenate([tab, tab], axis=-1)


def _block_diag_pairs(pool_w):
    z = jnp.zeros((LANES, LANES), pool_w.dtype)
    pairs = []
    for pp in range(2):
        top = jnp.concatenate([pool_w[2 * pp], z], axis=1)
        bot = jnp.concatenate([z, pool_w[2 * pp + 1]], axis=1)
        pairs.append(jnp.concatenate([top, bot], axis=0))
    return jnp.stack(pairs)


def kernel(x_prompt, x_sample, norm_g, ffn_w_gate, ffn_w_up, ffn_w_down, w_in_ab, pool_w, pool_scale, short_w, short_b, filt_w1, filt_b1, filt_w2, filt_b2, filt_w3, filt_b3, filt_freq, filt_w_out, filt_deltas, hyena_bias, w_out_ab, w_qkv, q_norm_g, k_norm_g, rpb, w_o):
    wg = ffn_w_gate.astype(BF16)
    wu = ffn_w_up.astype(BF16)
    wd = ffn_w_down.astype(BF16)

    def run_ffn(t, layer, slot, pre=()):
        return ffn(t, norm_g[layer, 2 * slot][None], wg, wu, wd, layer, slot, pre)

    w_in = w_in_ab[0].astype(BF16)
    pool_w2 = _block_diag_pairs(pool_w[0]).astype(BF16)
    w_qkv_b = w_qkv[0].astype(BF16)
    w_o_b = w_o[0].astype(BF16)
    gq = jnp.tile(q_norm_g[0], N_HEADS)[None]
    gk = jnp.tile(k_norm_g[0], N_HEADS)[None]

    fwd_np, inv_np = _dft_matrices()
    fwd_mats = [jnp.asarray(m).astype(BF16) for m in fwd_np]
    inv_mats = [jnp.asarray(m).astype(BF16) for m in inv_np]
    emb_np, tnorm_np = _filter_embedding()
    w1 = jnp.zeros((LANES, HYENA_HIDDEN), BF16).at[:HYENA_EMB].set(filt_w1[0].astype(BF16))
    spectra = hyena_filter(jnp.asarray(emb_np).astype(F32), jnp.asarray(tnorm_np).astype(F32), w1, filt_b1[0][None],
                           filt_w2[0].astype(BF16), filt_b2[0][None], filt_w3[0].astype(BF16), filt_b3[0][None],
                           filt_freq[0][None], filt_w_out[0].astype(BF16), filt_deltas[0][None], fwd_mats)
    table = _attn_bias_table(rpb[0])
    w_out_a = w_out_ab[0, :POOL_WIDTH].astype(BF16)
    w_out_h = w_out_ab[0, POOL_WIDTH:].astype(BF16)

    outs = []
    for x in (x_prompt, x_sample):
        b = x.shape[0]
        n = b * SEQ
        t = run_ffn(x.reshape(n, D_MODEL), 0, 0)
        a, z, x0 = mix0_in(t.reshape(b, SEQ, D_MODEL), norm_g[0, 1][None], w_in, pool_w2, pool_scale[0][None],
                           short_w[0], short_b[0][None])
        yh = hyena_conv(z, x0, hyena_bias[0][None], spectra, fwd_mats, inv_mats)
        t = run_ffn(t, 0, 1, ((a.reshape(n, POOL_WIDTH), w_out_a), (yh.reshape(n, HYENA_WIDTH), w_out_h)))
        t = run_ffn(t, 1, 0)
        q, k, v = qkv_proj(t, norm_g[1, 1][None], w_qkv_b, gq, gk)
        o = attention(q.reshape(b, SEQ, D_MODEL), k.reshape(b, SEQ, D_MODEL), v.reshape(b, SEQ, D_MODEL), table)
        t = run_ffn(t, 1, 1, ((o.reshape(n, D_MODEL), w_o_b),))
        outs.append(t.reshape(b, SEQ, D_MODEL))
    return tuple(outs)
```

```python
import functools
import math

import jax
import jax.numpy as jnp
import numpy as np
from jax import lax
from jax.experimental import pallas as pl
from jax.experimental.pallas import tpu as pltpu

BF16 = jnp.bfloat16
F32 = jnp.float32

D_MODEL = 1024
SEQ = 2048
D_FF = 2816
EPS = 1e-6
GRID_W = 64
POOL_WIDTH = 512
POOL_WINDOWS = (2, 4, 8, 16)
HYENA_WIDTH = 512
HYENA_BANDS = 16
HYENA_EMB = 1 + 2 * HYENA_BANDS
HYENA_HIDDEN = 64
HYENA_MOD_SHIFT = 0.05
N_HEADS = 16
HEAD_DIM = 64
NA_ROWS = 8
NA_COLS = 16
ROWS = SEQ // GRID_W

LANES = 128
MXU_N = 256
VMEM_LIMIT_BYTES = 56 * 1024 * 1024

FFN_TOKENS = 512
FFN_SUB = 2
FFN_CHUNK = MXU_N

HY_BLOCKS = 2
HY_T = SEQ // HY_BLOCKS
HY_FILTERS = 2 * HY_BLOCKS - 1
DFT_N = 2 * HY_T
DFT_H = HY_T // 2
DFT_FP = DFT_H + 8
POOL_PAD = 8

ATT_QROWS = 2
ATT_KROWS = 10
ATT_Q = ATT_QROWS * GRID_W
ATT_K = ATT_KROWS * GRID_W
ATT_GROUPS = ROWS // ATT_QROWS
ATT_TYPES = 5
ATT_DR_INVALID = 2 * NA_ROWS - 1
ATT_ONES = 16
ATT_PAIRS = 4
ATT_STAGE_LAG = 2
NEG_BIG = -1e30
LOG2_E = math.log2(math.e)
ATT_Q_SCALE = HEAD_DIM ** -0.5 * LOG2_E


def _params(n_axes=1):
    return pltpu.CompilerParams(dimension_semantics=("parallel",) * n_axes,
                                vmem_limit_bytes=VMEM_LIMIT_BYTES)


def _resident(shape, n_axes=1):
    zeros = (0,) * len(shape)
    if n_axes == 1:
        return pl.BlockSpec(shape, lambda i: zeros, pipeline_mode=pl.Buffered(1))
    return pl.BlockSpec(shape, lambda i, j: zeros, pipeline_mode=pl.Buffered(1))


def _dot(a, b):
    return jnp.dot(a, b, preferred_element_type=F32)


def _rms_scale(x, g):
    ms = jnp.mean(x * x, axis=-1, keepdims=True)
    return x * lax.rsqrt(ms + EPS) * g


def _ffn_kernel(n_pre, x_ref, *refs):
    pre = refs[:2 * n_pre]
    g_ref, wg_ref, wu_ref, wd_ref, o_ref = refs[2 * n_pre:]
    for sub in range(FFN_SUB):
        rows = slice(sub * FFN_TOKENS, (sub + 1) * FFN_TOKENS)
        x = x_ref[rows, :]
        for i in range(n_pre):
            x = x + _dot(pre[2 * i][rows, :], pre[2 * i + 1][...])
        xn = _rms_scale(x, g_ref[...]).astype(BF16)
        acc = jnp.zeros(x.shape, F32)
        for c in range(D_FF // FFN_CHUNK):
            cols = slice(c * FFN_CHUNK, (c + 1) * FFN_CHUNK)
            gate = _dot(xn, wg_ref[:, cols])
            up = _dot(xn, wu_ref[:, cols])
            h = (gate / (1.0 + jnp.exp(-gate)) * up).astype(BF16)
            acc = acc + _dot(h, wd_ref[cols, :])
        o_ref[rows, :] = x + 0.5 * acc


def _layer_slot(shape, layer, slot):
    return pl.BlockSpec((None, None) + shape, lambda i: (layer, slot, 0, 0), pipeline_mode=pl.Buffered(1))


def ffn(x, g, wg, wu, wd, layer, slot, pre=()):
    n = x.shape[0]
    step = FFN_SUB * FFN_TOKENS
    tile = pl.BlockSpec((step, D_MODEL), lambda i: (i, 0))
    in_specs, args = [tile], [x]
    for act, w in pre:
        in_specs += [pl.BlockSpec((step, act.shape[1]), lambda i: (i, 0)), _resident(w.shape)]
        args += [act, w]
    in_specs += [_resident((1, D_MODEL)), _layer_slot((D_MODEL, D_FF), layer, slot),
                 _layer_slot((D_MODEL, D_FF), layer, slot), _layer_slot((D_FF, D_MODEL), layer, slot)]
    args += [g, wg, wu, wd]
    return pl.pallas_call(
        functools.partial(_ffn_kernel, len(pre)),
        out_shape=jax.ShapeDtypeStruct(x.shape, F32),
        grid=(n // step,),
        in_specs=in_specs,
        out_specs=tile,
        compiler_params=_params(),
        name="ffn",
    )(*args)


def _shift_rows(x, k):
    return pltpu.roll(x, k % x.shape[0], 0)


def _window_sum(ue, w):
    p = ue + _shift_rows(ue, 1)
    if w == 2:
        return p
    span = 2
    while 2 * span < w:
        p = p + _shift_rows(p, span)
        span *= 2
    return _shift_rows(p, 1) + _shift_rows(p, -(span - 1))


MIX0_SLOTS = 3


def _mix0_in_kernel(x_ref, g_ref, win_ref, pw_ref, ps_ref, sw_ref, sb_ref, a_ref, z_ref, x0_ref, p_ref, x1_ref):
    xn = _rms_scale(x_ref[0], g_ref[...]).astype(BF16)
    row8 = lax.broadcasted_iota(jnp.int32, (POOL_PAD, 1), 0)
    first_row = row8 == 0
    last_row = row8 == POOL_PAD - 1
    n_pool = POOL_WIDTH // MXU_N
    n_slices = (POOL_WIDTH + 3 * HYENA_WIDTH) // MXU_N
    body = slice(POOL_PAD, POOL_PAD + SEQ)

    zeros = jnp.zeros((POOL_PAD, MXU_N), F32)
    for slot in range(MIX0_SLOTS):
        p_ref[slot, 0:POOL_PAD, :] = zeros
        p_ref[slot, POOL_PAD + SEQ:POOL_PAD + SEQ + POOL_PAD, :] = zeros

    def project(k):
        p_ref[k % MIX0_SLOTS, body, :] = _dot(xn, win_ref[:, k * MXU_N:(k + 1) * MXU_N])

    def pool(k):
        ue = p_ref[k % MIX0_SLOTS]
        halves = []
        for j in range(MXU_N // LANES):
            w = POOL_WINDOWS[k * (MXU_N // LANES) + j]
            lanes = slice(j * LANES, (j + 1) * LANES)
            win = _window_sum(ue[:, lanes], w)[body]
            uj = ue[body, lanes]

            def edge(r0, win=win, uj=uj, w=w):
                rr = r0 + row8
                cnt = (jnp.minimum(rr + (w - w // 2 - 1), SEQ - 1) - jnp.maximum(rr - w // 2, 0) + 1).astype(F32)
                return win[r0:r0 + POOL_PAD] / cnt - uj[r0:r0 + POOL_PAD]

            inner = win[POOL_PAD:SEQ - POOL_PAD] * (1.0 / w) - uj[POOL_PAD:SEQ - POOL_PAD]
            halves.append(jnp.concatenate([edge(0), inner, edge(SEQ - POOL_PAD)], axis=0))
        pooled = jnp.concatenate(halves, axis=1).astype(BF16)
        mixed = _dot(pooled, pw_ref[k]) * ps_ref[:, k * MXU_N:(k + 1) * MXU_N]
        a_ref[0, :, k * MXU_N:(k + 1) * MXU_N] = mixed.astype(BF16)

    def short_conv(k):
        c0 = (k - n_pool) * MXU_N
        p = p_ref[k % MIX0_SLOTS, body, :]
        w = sw_ref[:, c0:c0 + MXU_N]
        bias = sb_ref[:, c0:c0 + MXU_N]
        prev = _shift_rows(p, 1)
        nxt = _shift_rows(p, -1)
        conv = prev * w[0:1] + p * w[1:2] + nxt * w[2:3] + bias
        top = slice(0, POOL_PAD)
        bot = slice(SEQ - POOL_PAD, SEQ)
        conv_top = jnp.where(first_row, 0.0, prev[top]) * w[0:1] + p[top] * w[1:2] + nxt[top] * w[2:3] + bias
        conv_bot = prev[bot] * w[0:1] + p[bot] * w[1:2] + jnp.where(last_row, 0.0, nxt[bot]) * w[2:3] + bias
        return jnp.concatenate([conv_top, conv[POOL_PAD:SEQ - POOL_PAD], conv_bot], axis=0)

    def finish(k):
        if k < n_pool:
            pool(k)
            return
        part, cc = divmod(k - n_pool, HYENA_WIDTH // MXU_N)
        conv = short_conv(k)
        if part == 0:
            x0_ref[0, :, cc * MXU_N:(cc + 1) * MXU_N] = conv
        elif part == 1:
            x1_ref[cc] = conv
        else:
            zc = conv * x1_ref[cc]
            for j in range(MXU_N // LANES):
                z_ref[0, cc * (MXU_N // LANES) + j] = zc[:, j * LANES:(j + 1) * LANES]

    project(0)
    for k in range(1, n_slices):
        project(k)
        finish(k - 1)
    finish(n_slices - 1)


def mix0_in(x, g, w_in, pool_w2, pool_scale, short_w, short_b):
    b = x.shape[0]
    return pl.pallas_call(
        _mix0_in_kernel,
        out_shape=(jax.ShapeDtypeStruct((b, SEQ, POOL_WIDTH), BF16),
                   jax.ShapeDtypeStruct((b, HYENA_WIDTH // LANES, SEQ, LANES), F32),
                   jax.ShapeDtypeStruct((b, SEQ, HYENA_WIDTH), F32)),
        grid=(b,),
        in_specs=[
            pl.BlockSpec((1, SEQ, D_MODEL), lambda i: (i, 0, 0)),
            _resident((1, D_MODEL)),
            _resident(w_in.shape),
            _resident(pool_w2.shape),
            _resident((1, POOL_WIDTH)),
            _resident(short_w.shape),
            _resident(short_b.shape),
        ],
        out_specs=(pl.BlockSpec((1, SEQ, POOL_WIDTH), lambda i: (i, 0, 0)),
                   pl.BlockSpec((1, HYENA_WIDTH // LANES, SEQ, LANES), lambda i: (i, 0, 0, 0)),
                   pl.BlockSpec((1, SEQ, HYENA_WIDTH), lambda i: (i, 0, 0))),
        scratch_shapes=[pltpu.VMEM((MIX0_SLOTS, SEQ + 2 * POOL_PAD, MXU_N), F32),
                        pltpu.VMEM((HYENA_WIDTH // MXU_N, SEQ, MXU_N), F32)],
        compiler_params=_params(),
        name="mix0_in",
    )(x, g, w_in, pool_w2, pool_scale, short_w, short_b)


def _dft_matrices():
    f = np.arange(DFT_FP, dtype=np.int64)[:, None]
    s = np.arange(DFT_H, dtype=np.int64)[None, :]
    keep = f <= DFT_H
    unit = 2.0 * np.pi / DFT_N
    ang_e = unit * ((f * (2 * s)) % DFT_N)
    ang_o = unit * ((f * (2 * s + 1)) % DFT_N)
    fwd = [np.where(keep, m, 0.0) for m in (np.cos(ang_e), np.sin(ang_e), np.cos(ang_o), np.sin(ang_o))]
    t = np.arange(DFT_H, dtype=np.int64)[:, None]
    fi = np.arange(DFT_H, dtype=np.int64)[None, :]
    wgt = np.where(fi == 0, 1.0, 2.0) / DFT_N
    ph_e = unit * ((fi * (2 * t)) % DFT_N)
    ph_o = unit * ((fi * (2 * t + 1)) % DFT_N)
    inv = [wgt * np.cos(ph_e), -wgt * np.sin(ph_e), wgt * np.cos(ph_o), -wgt * np.sin(ph_o)]
    return fwd, inv


def _filter_positions():
    m = np.arange(HY_T)
    return np.stack([m, HY_T + m, HY_T - m])


def _filter_embedding():
    pos = _filter_positions().reshape(-1)
    t_norm = np.linspace(0.0, 1.0, SEQ, dtype=np.float64)[pos]
    bands = np.linspace(1e-4, HYENA_BANDS - 1, HYENA_BANDS, dtype=np.float64)
    ang = (2.0 * math.pi / SEQ) * pos.astype(np.float64)[:, None] * bands[None, :]
    out = np.zeros((pos.shape[0], LANES), np.float64)
    out[:, :HYENA_EMB] = np.concatenate([t_norm[:, None], np.cos(ang), -np.sin(ang)], axis=-1)
    return out, t_norm[:, None]


_BLOCK_FILTER_SOURCES = (
    ((2, 1), (1, 1)),
    ((0, 0), (0, 1)),
    ((1, 0), (2, 0)),
)


def _filter_kernel(emb_ref, tn_ref, w1_ref, b1_ref, w2_ref, b2_ref, w3_ref, b3_ref, fr_ref, wo_ref, dl_ref,
                   ce_ref, se_ref, co_ref, so_ref, k1r_ref, k1i_ref, k2r_ref, k2i_ref, h_ref):
    fr = fr_ref[...]
    n_chunks = HYENA_WIDTH // LANES
    for ps in range(3):
        rows = slice(ps * HY_T, (ps + 1) * HY_T)
        h = jnp.sin(fr * (_dot(emb_ref[rows, :].astype(BF16), w1_ref[...]) + b1_ref[...]))
        h = jnp.sin(fr * (_dot(h.astype(BF16), w2_ref[...]) + b2_ref[...]))
        h = jnp.sin(fr * (_dot(h.astype(BF16), w3_ref[...]) + b3_ref[...]))
        ho = _dot(h.astype(BF16), wo_ref[...])
        decay = jnp.exp(-tn_ref[rows, :] * jnp.abs(dl_ref[...]))
        hm = ho * (decay + HYENA_MOD_SHIFT)
        for direction in range(2):
            for c in range(n_chunks):
                c0 = direction * HYENA_WIDTH + c * LANES
                h_ref[2 * ps + direction, c] = hm[:, c0:c0 + LANES]

    first = lax.broadcasted_iota(jnp.int32, (DFT_H, 1), 0) == 0

    def parity(src, par):
        ps, direction = src
        return jnp.concatenate([h_ref[2 * ps + direction, c, pl.ds(par, DFT_H, stride=2), :]
                                for c in range(n_chunks)], axis=1)

    for fi, (fwd_src, bwd_src) in enumerate(_BLOCK_FILTER_SOURCES):
        parts = []
        for par, (cos_ref, sin_ref) in enumerate(((ce_ref, se_ref), (co_ref, so_ref))):
            fwd = parity(fwd_src, par)
            bwd = parity(bwd_src, par)
            if par == 0:
                bwd = jnp.where(first, 0.0, bwd)
            parts.append((_dot(cos_ref[...], (fwd + bwd).astype(BF16)), _dot(sin_ref[...], (bwd - fwd).astype(BF16))))
        (even_re, even_im), (odd_re, odd_im) = parts
        k1r_ref[fi] = even_re + odd_re
        k1i_ref[fi] = even_im + odd_im
        k2r_ref[fi] = even_re - odd_re
        k2i_ref[fi] = even_im - odd_im


def hyena_filter(emb, tnorm, w1, b1, w2, b2, w3, b3, freq, w_out, deltas, fwd_mats):
    spec = jax.ShapeDtypeStruct((HY_FILTERS, DFT_FP, HYENA_WIDTH), F32)
    return pl.pallas_call(
        _filter_kernel,
        out_shape=(spec,) * 4,
        scratch_shapes=[pltpu.VMEM((6, HYENA_WIDTH // LANES, HY_T, LANES), F32)],
        compiler_params=pltpu.CompilerParams(vmem_limit_bytes=VMEM_LIMIT_BYTES),
        name="hyena_filter",
    )(emb, tnorm, w1, b1, w2, b2, w3, b3, freq, w_out, deltas, *fwd_mats)


def _hyena_conv_kernel(z_ref, x0_ref, hb_ref, k1r_ref, k1i_ref, k2r_ref, k2i_ref,
                       ce_ref, se_ref, co_ref, so_ref, ice_ref, ise_ref, ico_ref, iso_ref, o_ref, y_ref):
    n_chunks = MXU_N // LANES

    def samples(par):
        return jnp.concatenate([z_ref[0, c, pl.ds(j * HY_T + par, DFT_H, stride=2), :]
                                for j in range(HY_BLOCKS) for c in range(n_chunks)], axis=1).astype(BF16)

    ze = samples(0)
    zo = samples(1)
    er = _dot(ce_ref[...], ze)
    ei = _dot(se_ref[...], ze)
    orr = _dot(co_ref[...], zo)
    oi = _dot(so_ref[...], zo)
    x1r, x1i, x2r, x2i = er + orr, ei + oi, er - orr, ei - oi
    a_r, a_i, d_r, d_i = [], [], [], []
    for i in range(HY_BLOCKS):
        y1r = y1i = y2r = y2i = 0.0
        for j in range(HY_BLOCKS):
            lanes = slice(j * MXU_N, (j + 1) * MXU_N)
            fi = i - j + HY_BLOCKS - 1
            k1r, k1i, k2r, k2i = k1r_ref[fi], k1i_ref[fi], k2r_ref[fi], k2i_ref[fi]
            y1r = y1r + (x1r[:, lanes] * k1r + x1i[:, lanes] * k1i)
            y1i = y1i + (x1r[:, lanes] * k1i - x1i[:, lanes] * k1r)
            y2r = y2r + (x2r[:, lanes] * k2r + x2i[:, lanes] * k2i)
            y2i = y2i + (x2r[:, lanes] * k2i - x2i[:, lanes] * k2r)
        a_r.append(y1r + y2r)
        a_i.append(y1i + y2i)
        d_r.append(y1r - y2r)
        d_i.append(y1i - y2i)
    a_r, a_i, d_r, d_i = [jnp.concatenate(v, axis=1) for v in (a_r, a_i, d_r, d_i)]
    trow = lax.broadcasted_iota(jnp.int32, (DFT_H, 1), 0)
    sign = jnp.where((trow & 1) == 0, 1.0 / DFT_N, -1.0 / DFT_N)
    ye = (_dot(ice_ref[...], a_r[:DFT_H].astype(BF16)) + _dot(ise_ref[...], a_i[:DFT_H].astype(BF16))
          + sign * a_r[DFT_H:DFT_H + 1])
    yo = (_dot(ico_ref[...], d_r[:DFT_H].astype(BF16)) + _dot(iso_ref[...], d_i[:DFT_H].astype(BF16))
          - sign * d_i[DFT_H:DFT_H + 1])
    for i in range(HY_BLOCKS):
        for c in range(n_chunks):
            lanes = slice(i * MXU_N + c * LANES, i * MXU_N + (c + 1) * LANES)
            y_ref[c, pl.ds(i * HY_T, DFT_H, stride=2), :] = ye[:, lanes]
            y_ref[c, pl.ds(i * HY_T + 1, DFT_H, stride=2), :] = yo[:, lanes]
    for c in range(n_chunks):
        lanes = slice(c * LANES, (c + 1) * LANES)
        y = y_ref[c] + z_ref[0, c] * hb_ref[:, lanes]
        o_ref[0, :, lanes] = (y * x0_ref[0, :, lanes]).astype(BF16)


def hyena_conv(z, x0, h_bias, spectra, fwd_mats, inv_mats):
    b = z.shape[0]
    n_cc = HYENA_WIDTH // MXU_N
    spec_block = pl.BlockSpec((HY_FILTERS, DFT_FP, MXU_N), lambda c, i: (0, 0, c))
    return pl.pallas_call(
        _hyena_conv_kernel,
        out_shape=jax.ShapeDtypeStruct((b, SEQ, HYENA_WIDTH), BF16),
        grid=(n_cc, b),
        in_specs=[
            pl.BlockSpec((1, MXU_N // LANES, SEQ, LANES), lambda c, i: (i, c, 0, 0)),
            pl.BlockSpec((1, SEQ, MXU_N), lambda c, i: (i, 0, c)),
            pl.BlockSpec((1, MXU_N), lambda c, i: (0, c)),
            spec_block, spec_block, spec_block, spec_block,
        ] + [_resident(m.shape, 2) for m in fwd_mats + inv_mats],
        out_specs=pl.BlockSpec((1, SEQ, MXU_N), lambda c, i: (i, 0, c)),
        scratch_shapes=[pltpu.VMEM((MXU_N // LANES, SEQ, LANES), F32)],
        compiler_params=_params(2),
        name="hyena_conv",
    )(z, x0, h_bias, *spectra, *fwd_mats, *inv_mats)


def _head_rms(t, g):
    low = lax.broadcasted_iota(jnp.int32, (1, LANES), 1) < HEAD_DIM
    blocks = []
    for j in range(D_MODEL // LANES):
        blk = t[:, j * LANES:(j + 1) * LANES]
        sq = blk * blk
        s_lo = jnp.sum(jnp.where(low, sq, 0.0), axis=-1, keepdims=True)
        s_hi = jnp.sum(jnp.where(low, 0.0, sq), axis=-1, keepdims=True)
        ms = jnp.where(low, s_lo, s_hi) * (1.0 / HEAD_DIM)
        blocks.append(blk * lax.rsqrt(ms + EPS))
    return jnp.concatenate(blocks, axis=1) * g


def _qkv_kernel(x_ref, g_ref, w_ref, gq_ref, gk_ref, q_ref, k_ref, v_ref):
    for sub in range(FFN_SUB):
        rows = slice(sub * FFN_TOKENS, (sub + 1) * FFN_TOKENS)
        xn = _rms_scale(x_ref[rows, :], g_ref[...]).astype(BF16)
        q = _dot(xn, w_ref[:, 0:D_MODEL])
        q_ref[rows, :] = (_head_rms(q, gq_ref[...]) * ATT_Q_SCALE).astype(BF16)
        k = _dot(xn, w_ref[:, D_MODEL:2 * D_MODEL])
        k_ref[rows, :] = _head_rms(k, gk_ref[...]).astype(BF16)
        v_ref[rows, :] = _dot(xn, w_ref[:, 2 * D_MODEL:3 * D_MODEL]).astype(BF16)


def qkv_proj(x, g, w_qkv, gq, gk):
    n = x.shape[0]
    step = FFN_SUB * FFN_TOKENS
    tile = pl.BlockSpec((step, D_MODEL), lambda i: (i, 0))
    out = jax.ShapeDtypeStruct((n, D_MODEL), BF16)
    return pl.pallas_call(
        _qkv_kernel,
        out_shape=(out, out, out),
        grid=(n // step,),
        in_specs=[tile, _resident((1, D_MODEL)), _resident(w_qkv.shape), _resident((1, D_MODEL)),
                  _resident((1, D_MODEL))],
        out_specs=(tile, tile, tile),
        compiler_params=_params(),
        name="qkv_proj",
    )(x, g, w_qkv, gq, gk)


def _att_group_start(gi):
    return min(max(ATT_QROWS * gi - NA_ROWS // 2, 0), ROWS - ATT_KROWS)


def _att_group_type(gi):
    if gi < 2:
        return gi
    if gi >= ATT_GROUPS - 2:
        return ATT_TYPES - (ATT_GROUPS - gi)
    return 2


def _att_row_offsets():
    dr = np.full((ATT_TYPES, ATT_QROWS, ATT_KROWS), ATT_DR_INVALID, np.int32)
    reps = {0: 0, 1: 1, 2: 2, 3: ATT_GROUPS - 2, 4: ATT_GROUPS - 1}
    for ty, gi in reps.items():
        start = _att_group_start(gi)
        for j in range(ATT_QROWS):
            qrow = ATT_QROWS * gi + j
            s_q = min(max(qrow - NA_ROWS // 2, 0), ROWS - NA_ROWS)
            for i in range(ATT_KROWS):
                krow = start + i
                if s_q <= krow < s_q + NA_ROWS:
                    dr[ty, j, i] = krow - qrow + NA_ROWS - 1
    return dr


def _attn_kernel(q_ref, k_ref, v_ref, tab_ref, o_ref, bias_ref, vt_ref, s_ref, p_ref):
    low = lax.broadcasted_iota(jnp.int32, (1, LANES), 1) < HEAD_DIM
    dr = _att_row_offsets()

    @pl.when(pl.program_id(1) == 0)
    def _build_bias():
        for hp in range(ATT_PAIRS):
            for ty in range(ATT_TYPES):
                for i in range(ATT_KROWS):
                    for hh in range(2):
                        h = 2 * hp + hh
                        tile = jnp.where(low, tab_ref[h, int(dr[ty, 0, i])], tab_ref[h, int(dr[ty, 1, i])])
                        bias_ref[hp, ty, i * GRID_W:(i + 1) * GRID_W, hh * LANES:(hh + 1) * LANES] = tile

    for hp in range(ATT_PAIRS):
        vt_ref[hp, 0:LANES, :] = v_ref[0, :, hp * LANES:(hp + 1) * LANES].T
        vt_ref[hp, LANES:LANES + ATT_ONES, :] = jnp.ones((ATT_ONES, SEQ), BF16)

    def scores(hp, gi, slot):
        q0 = gi * ATT_Q
        k0 = _att_group_start(gi) * GRID_W
        lanes = slice(hp * LANES, (hp + 1) * LANES)
        qg = q_ref[0, q0:q0 + ATT_Q, lanes]
        zero = jnp.zeros_like(qg)
        qs = jnp.concatenate([jnp.where(low, qg, zero), jnp.where(low, zero, qg)], axis=0)
        st = lax.dot_general(k_ref[0, k0:k0 + ATT_K, lanes], qs, (((1,), (1,)), ((), ())),
                             preferred_element_type=F32)
        s_ref[slot] = st + bias_ref[hp, _att_group_type(gi)]

    def softmax(hp, gi, slot):
        s = s_ref[slot]
        m = jnp.max(s, axis=0, keepdims=True)
        p_ref[slot] = jnp.exp2(s - m).astype(BF16)

    def values(hp, gi, slot):
        q0 = gi * ATT_Q
        k0 = _att_group_start(gi) * GRID_W
        r = _dot(vt_ref[hp, :, k0:k0 + ATT_K], p_ref[slot])
        den = r[LANES:LANES + 1]
        o0 = r[0:HEAD_DIM, 0:LANES] / den[:, 0:LANES]
        o1 = r[HEAD_DIM:LANES, LANES:2 * LANES] / den[:, LANES:2 * LANES]
        o_ref[0, q0:q0 + ATT_Q, hp * LANES:(hp + 1) * LANES] = jnp.concatenate([o0, o1], axis=0).T.astype(BF16)

    items = [(hp, gi) for hp in range(ATT_PAIRS) for gi in range(ATT_GROUPS)]
    for step in range(len(items) + 2 * ATT_STAGE_LAG):
        for stage, fn in enumerate((scores, softmax, values)):
            n = step - stage * ATT_STAGE_LAG
            if 0 <= n < len(items):
                fn(*items[n], n % ATT_GROUPS)


def attention(q, k, v, table):
    b = q.shape[0]
    width = ATT_PAIRS * LANES
    blk = pl.BlockSpec((1, SEQ, width), lambda h, i: (i, 0, h))
    return pl.pallas_call(
        _attn_kernel,
        out_shape=jax.ShapeDtypeStruct((b, SEQ, D_MODEL), BF16),
        grid=(D_MODEL // width, b),
        in_specs=[blk, blk, blk,
                  pl.BlockSpec((2 * ATT_PAIRS, ATT_DR_INVALID + 1, GRID_W, LANES), lambda h, i: (h, 0, 0, 0))],
        out_specs=blk,
        scratch_shapes=[pltpu.VMEM((ATT_PAIRS, ATT_TYPES, ATT_K, 2 * ATT_Q), F32),
                        pltpu.VMEM((ATT_PAIRS, LANES + ATT_ONES, SEQ), BF16),
                        pltpu.VMEM((ATT_GROUPS, ATT_K, 2 * ATT_Q), F32),
                        pltpu.VMEM((ATT_GROUPS, ATT_K, 2 * ATT_Q), BF16)],
        compiler_params=pltpu.CompilerParams(dimension_semantics=("arbitrary", "arbitrary"),
                                             vmem_limit_bytes=VMEM_LIMIT_BYTES),
        name="attention",
    )(q, k, v, table)


def _attn_bias_table(rpb):
    c = np.arange(GRID_W)
    c_start = np.clip(c - NA_COLS // 2, 0, GRID_W - NA_COLS)
    col_ok = (c[None, :] >= c_start[:, None]) & (c[None, :] < c_start[:, None] + NA_COLS)
    dc = np.clip(c[None, :] - c[:, None] + NA_COLS - 1, 0, 2 * NA_COLS - 2)
    onehot = (dc.T[None] == np.arange(2 * NA_COLS - 1)[:, None, None]).astype(np.float32)
    tab = jnp.einsum("hrd,dkq->hrkq", rpb.astype(F32), jnp.asarray(onehot), precision=lax.Precision.HIGHEST)
    tab = jnp.where(jnp.asarray(col_ok.T)[None, None], tab * LOG2_E, NEG_BIG)
    tab = jnp.concatenate([tab, jnp.full((N_HEADS, 1, GRID_W, GRID_W), NEG_BIG, F32)], axis=1)
    return jnp.concatenate([tab, tab], axis=-1)


def _block_diag_pairs(pool_w):
    z = jnp.zeros((LANES, LANES), pool_w.dtype)
    pairs = []
    for pp in range(2):
        top = jnp.concatenate([pool_w[2 * pp], z], axis=1)
        bot = jnp.concatenate([z, pool_w[2 * pp + 1]], axis=1)
        pairs.append(jnp.concatenate([top, bot], axis=0))
    return jnp.stack(pairs)


def kernel(x_prompt, x_sample, norm_g, ffn_w_gate, ffn_w_up, ffn_w_down, w_in_ab, pool_w, pool_scale, short_w, short_b, filt_w1, filt_b1, filt_w2, filt_b2, filt_w3, filt_b3, filt_freq, filt_w_out, filt_deltas, hyena_bias, w_out_ab, w_qkv, q_norm_g, k_norm_g, rpb, w_o):
    wg = ffn_w_gate.astype(BF16)
    wu = ffn_w_up.astype(BF16)
    wd = ffn_w_down.astype(BF16)

    def run_ffn(t, layer, slot, pre=()):
        return ffn(t, norm_g[layer, 2 * slot][None], wg, wu, wd, layer, slot, pre)

    w_in = w_in_ab[0].astype(BF16)
    pool_w2 = _block_diag_pairs(pool_w[0]).astype(BF16)
    w_qkv_b = w_qkv[0].astype(BF16)
    w_o_b = w_o[0].astype(BF16)
    gq = jnp.tile(q_norm_g[0], N_HEADS)[None]
    gk = jnp.tile(k_norm_g[0], N_HEADS)[None]

    fwd_np, inv_np = _dft_matrices()
    fwd_mats = [jnp.asarray(m).astype(BF16) for m in fwd_np]
    inv_mats = [jnp.asarray(m).astype(BF16) for m in inv_np]
    emb_np, tnorm_np = _filter_embedding()
    w1 = jnp.zeros((LANES, HYENA_HIDDEN), BF16).at[:HYENA_EMB].set(filt_w1[0].astype(BF16))
    spectra = hyena_filter(jnp.asarray(emb_np).astype(F32), jnp.asarray(tnorm_np).astype(F32), w1, filt_b1[0][None],
                           filt_w2[0].astype(BF16), filt_b2[0][None], filt_w3[0].astype(BF16), filt_b3[0][None],
                           filt_freq[0][None], filt_w_out[0].astype(BF16), filt_deltas[0][None], fwd_mats)
    table = _attn_bias_table(rpb[0])
    w_out_a = w_out_ab[0, :POOL_WIDTH].astype(BF16)
    w_out_h = w_out_ab[0, POOL_WIDTH:].astype(BF16)

    outs = []
    for x in (x_prompt, x_sample):
        b = x.shape[0]
        n = b * SEQ
        t = run_ffn(x.reshape(n, D_MODEL), 0, 0)
        a, z, x0 = mix0_in(t.reshape(b, SEQ, D_MODEL), norm_g[0, 1][None], w_in, pool_w2, pool_scale[0][None],
                           short_w[0], short_b[0][None])
        yh = hyena_conv(z, x0, hyena_bias[0][None], spectra, fwd_mats, inv_mats)
        t = run_ffn(t, 0, 1, ((a.reshape(n, POOL_WIDTH), w_out_a), (yh.reshape(n, HYENA_WIDTH), w_out_h)))
        t = run_ffn(t, 1, 0)
        q, k, v = qkv_proj(t, norm_g[1, 1][None], w_qkv_b, gq, gk)
        o = attention(q.reshape(b, SEQ, D_MODEL), k.reshape(b, SEQ, D_MODEL), v.reshape(b, SEQ, D_MODEL), table)
        t = run_ffn(t, 1, 1, ((o.reshape(n, D_MODEL), w_o_b),))
        outs.append(t.reshape(b, SEQ, D_MODEL))
    return tuple(outs)
```

```python
import functools
import math

import jax
import jax.numpy as jnp
import numpy as np
from jax import lax
from jax.experimental import pallas as pl
from jax.experimental.pallas import tpu as pltpu

BF16 = jnp.bfloat16
F32 = jnp.float32

D_MODEL = 1024
SEQ = 2048
D_FF = 2816
EPS = 1e-6
GRID_W = 64
POOL_WIDTH = 512
POOL_WINDOWS = (2, 4, 8, 16)
HYENA_WIDTH = 512
HYENA_BANDS = 16
HYENA_EMB = 1 + 2 * HYENA_BANDS
HYENA_HIDDEN = 64
HYENA_MOD_SHIFT = 0.05
N_HEADS = 16
HEAD_DIM = 64
NA_ROWS = 8
NA_COLS = 16
ROWS = SEQ // GRID_W

LANES = 128
MXU_N = 256
VMEM_LIMIT_BYTES = 56 * 1024 * 1024

FFN_TOKENS = 512
FFN_SUB = 2
QKV_SUB = 4
FFN_CHUNK = MXU_N

HY_BLOCKS = 2
HY_T = SEQ // HY_BLOCKS
HY_FILTERS = 2 * HY_BLOCKS - 1
DFT_N = 2 * HY_T
DFT_H = HY_T // 2
DFT_FP = DFT_H + 8
POOL_PAD = 8

ATT_QROWS = 2
ATT_KROWS = 10
ATT_Q = ATT_QROWS * GRID_W
ATT_K = ATT_KROWS * GRID_W
ATT_GROUPS = ROWS // ATT_QROWS
ATT_TYPES = 5
ATT_DR_INVALID = 2 * NA_ROWS - 1
ATT_ONES = 16
ATT_PAIRS = 4
ATT_STAGE_LAG = 2
NEG_BIG = -1e30
LOG2_E = math.log2(math.e)
ATT_Q_SCALE = HEAD_DIM ** -0.5 * LOG2_E


def _params(n_axes=1):
    return pltpu.CompilerParams(dimension_semantics=("parallel",) * n_axes,
                                vmem_limit_bytes=VMEM_LIMIT_BYTES)


def _resident(shape, n_axes=1):
    zeros = (0,) * len(shape)
    if n_axes == 1:
        return pl.BlockSpec(shape, lambda i: zeros, pipeline_mode=pl.Buffered(1))
    return pl.BlockSpec(shape, lambda i, j: zeros, pipeline_mode=pl.Buffered(1))


def _dot(a, b):
    return jnp.dot(a, b, preferred_element_type=F32)


def _rms_scale(x, g):
    ms = jnp.mean(x * x, axis=-1, keepdims=True)
    return x * lax.rsqrt(ms + EPS) * g


def _ffn_kernel(n_pre, x_ref, *refs):
    pre = refs[:2 * n_pre]
    g_ref, wg_ref, wu_ref, wd_ref, o_ref = refs[2 * n_pre:]
    for sub in range(FFN_SUB):
        rows = slice(sub * FFN_TOKENS, (sub + 1) * FFN_TOKENS)
        x = x_ref[rows, :]
        for i in range(n_pre):
            x = x + _dot(pre[2 * i][rows, :], pre[2 * i + 1][...])
        xn = _rms_scale(x, g_ref[...]).astype(BF16)
        acc = jnp.zeros(x.shape, F32)
        for c in range(D_FF // FFN_CHUNK):
            cols = slice(c * FFN_CHUNK, (c + 1) * FFN_CHUNK)
            gate = _dot(xn, wg_ref[:, cols])
            up = _dot(xn, wu_ref[:, cols])
            h = (gate / (1.0 + jnp.exp(-gate)) * up).astype(BF16)
            acc = acc + _dot(h, wd_ref[cols, :])
        o_ref[rows, :] = x + 0.5 * acc


def _layer_slot(shape, layer, slot):
    return pl.BlockSpec((None, None) + shape, lambda i: (layer, slot, 0, 0), pipeline_mode=pl.Buffered(1))


def ffn(x, g, wg, wu, wd, layer, slot, pre=()):
    n = x.shape[0]
    step = FFN_SUB * FFN_TOKENS
    tile = pl.BlockSpec((step, D_MODEL), lambda i: (i, 0))
    in_specs, args = [tile], [x]
    for act, w in pre:
        in_specs += [pl.BlockSpec((step, act.shape[1]), lambda i: (i, 0)), _resident(w.shape)]
        args += [act, w]
    in_specs += [_resident((1, D_MODEL)), _layer_slot((D_MODEL, D_FF), layer, slot),
                 _layer_slot((D_MODEL, D_FF), layer, slot), _layer_slot((D_FF, D_MODEL), layer, slot)]
    args += [g, wg, wu, wd]
    return pl.pallas_call(
        functools.partial(_ffn_kernel, len(pre)),
        out_shape=jax.ShapeDtypeStruct(x.shape, F32),
        grid=(n // step,),
        in_specs=in_specs,
        out_specs=tile,
        compiler_params=_params(),
        name="ffn",
    )(*args)


def _shift_rows(x, k):
    return pltpu.roll(x, k % x.shape[0], 0)


def _window_sum(ue, w):
    p = ue + _shift_rows(ue, 1)
    if w == 2:
        return p
    span = 2
    while 2 * span < w:
        p = p + _shift_rows(p, span)
        span *= 2
    return _shift_rows(p, 1) + _shift_rows(p, -(span - 1))


MIX0_SLOTS = 3


def _mix0_in_kernel(x_ref, g_ref, win_ref, pw_ref, ps_ref, sw_ref, sb_ref, a_ref, z_ref, x0_ref, p_ref, x1_ref):
    xn = _rms_scale(x_ref[0], g_ref[...]).astype(BF16)
    row8 = lax.broadcasted_iota(jnp.int32, (POOL_PAD, 1), 0)
    first_row = row8 == 0
    last_row = row8 == POOL_PAD - 1
    n_pool = POOL_WIDTH // MXU_N
    n_slices = (POOL_WIDTH + 3 * HYENA_WIDTH) // MXU_N
    body = slice(POOL_PAD, POOL_PAD + SEQ)

    zeros = jnp.zeros((POOL_PAD, MXU_N), F32)
    for slot in range(MIX0_SLOTS):
        p_ref[slot, 0:POOL_PAD, :] = zeros
        p_ref[slot, POOL_PAD + SEQ:POOL_PAD + SEQ + POOL_PAD, :] = zeros

    def project(k):
        p_ref[k % MIX0_SLOTS, body, :] = _dot(xn, win_ref[:, k * MXU_N:(k + 1) * MXU_N])

    def pool(k):
        ue = p_ref[k % MIX0_SLOTS]
        halves = []
        for j in range(MXU_N // LANES):
            w = POOL_WINDOWS[k * (MXU_N // LANES) + j]
            lanes = slice(j * LANES, (j + 1) * LANES)
            win = _window_sum(ue[:, lanes], w)[body]
            uj = ue[body, lanes]

            def edge(r0, win=win, uj=uj, w=w):
                rr = r0 + row8
                cnt = (jnp.minimum(rr + (w - w // 2 - 1), SEQ - 1) - jnp.maximum(rr - w // 2, 0) + 1).astype(F32)
                return win[r0:r0 + POOL_PAD] / cnt - uj[r0:r0 + POOL_PAD]

            inner = win[POOL_PAD:SEQ - POOL_PAD] * (1.0 / w) - uj[POOL_PAD:SEQ - POOL_PAD]
            halves.append(jnp.concatenate([edge(0), inner, edge(SEQ - POOL_PAD)], axis=0))
        pooled = jnp.concatenate(halves, axis=1).astype(BF16)
        mixed = _dot(pooled, pw_ref[k]) * ps_ref[:, k * MXU_N:(k + 1) * MXU_N]
        a_ref[0, :, k * MXU_N:(k + 1) * MXU_N] = mixed.astype(BF16)

    def short_conv(k):
        c0 = (k - n_pool) * MXU_N
        p = p_ref[k % MIX0_SLOTS, body, :]
        w = sw_ref[:, c0:c0 + MXU_N]
        bias = sb_ref[:, c0:c0 + MXU_N]
        prev = _shift_rows(p, 1)
        nxt = _shift_rows(p, -1)
        conv = prev * w[0:1] + p * w[1:2] + nxt * w[2:3] + bias
        top = slice(0, POOL_PAD)
        bot = slice(SEQ - POOL_PAD, SEQ)
        conv_top = jnp.where(first_row, 0.0, prev[top]) * w[0:1] + p[top] * w[1:2] + nxt[top] * w[2:3] + bias
        conv_bot = prev[bot] * w[0:1] + p[bot] * w[1:2] + jnp.where(last_row, 0.0, nxt[bot]) * w[2:3] + bias
        return jnp.concatenate([conv_top, conv[POOL_PAD:SEQ - POOL_PAD], conv_bot], axis=0)

    def finish(k):
        if k < n_pool:
            pool(k)
            return
        part, cc = divmod(k - n_pool, HYENA_WIDTH // MXU_N)
        conv = short_conv(k)
        if part == 0:
            x0_ref[0, :, cc * MXU_N:(cc + 1) * MXU_N] = conv
        elif part == 1:
            x1_ref[cc] = conv
        else:
            zc = conv * x1_ref[cc]
            for j in range(MXU_N // LANES):
                z_ref[0, cc * (MXU_N // LANES) + j] = zc[:, j * LANES:(j + 1) * LANES]

    project(0)
    for k in range(1, n_slices):
        project(k)
        finish(k - 1)
    finish(n_slices - 1)


def mix0_in(x, g, w_in, pool_w2, pool_scale, short_w, short_b):
    b = x.shape[0]
    return pl.pallas_call(
        _mix0_in_kernel,
        out_shape=(jax.ShapeDtypeStruct((b, SEQ, POOL_WIDTH), BF16),
                   jax.ShapeDtypeStruct((b, HYENA_WIDTH // LANES, SEQ, LANES), F32),
                   jax.ShapeDtypeStruct((b, SEQ, HYENA_WIDTH), F32)),
        grid=(b,),
        in_specs=[
            pl.BlockSpec((1, SEQ, D_MODEL), lambda i: (i, 0, 0)),
            _resident((1, D_MODEL)),
            _resident(w_in.shape),
            _resident(pool_w2.shape),
            _resident((1, POOL_WIDTH)),
            _resident(short_w.shape),
            _resident(short_b.shape),
        ],
        out_specs=(pl.BlockSpec((1, SEQ, POOL_WIDTH), lambda i: (i, 0, 0)),
                   pl.BlockSpec((1, HYENA_WIDTH // LANES, SEQ, LANES), lambda i: (i, 0, 0, 0)),
                   pl.BlockSpec((1, SEQ, HYENA_WIDTH), lambda i: (i, 0, 0))),
        scratch_shapes=[pltpu.VMEM((MIX0_SLOTS, SEQ + 2 * POOL_PAD, MXU_N), F32),
                        pltpu.VMEM((HYENA_WIDTH // MXU_N, SEQ, MXU_N), F32)],
        compiler_params=_params(),
        name="mix0_in",
    )(x, g, w_in, pool_w2, pool_scale, short_w, short_b)


def _dft_matrices():
    f = np.arange(DFT_FP, dtype=np.int64)[:, None]
    s = np.arange(DFT_H, dtype=np.int64)[None, :]
    keep = f <= DFT_H
    unit = 2.0 * np.pi / DFT_N
    ang_e = unit * ((f * (2 * s)) % DFT_N)
    ang_o = unit * ((f * (2 * s + 1)) % DFT_N)
    fwd = [np.where(keep, m, 0.0) for m in (np.cos(ang_e), np.sin(ang_e), np.cos(ang_o), np.sin(ang_o))]
    t = np.arange(DFT_H, dtype=np.int64)[:, None]
    fi = np.arange(DFT_H, dtype=np.int64)[None, :]
    wgt = np.where(fi == 0, 1.0, 2.0) / DFT_N
    ph_e = unit * ((fi * (2 * t)) % DFT_N)
    ph_o = unit * ((fi * (2 * t + 1)) % DFT_N)
    inv = [wgt * np.cos(ph_e), -wgt * np.sin(ph_e), wgt * np.cos(ph_o), -wgt * np.sin(ph_o)]
    return fwd, inv


def _filter_positions():
    m = np.arange(HY_T)
    return np.stack([m, HY_T + m, HY_T - m])


def _filter_embedding():
    pos = _filter_positions().reshape(-1)
    t_norm = np.linspace(0.0, 1.0, SEQ, dtype=np.float64)[pos]
    bands = np.linspace(1e-4, HYENA_BANDS - 1, HYENA_BANDS, dtype=np.float64)
    ang = (2.0 * math.pi / SEQ) * pos.astype(np.float64)[:, None] * bands[None, :]
    out = np.zeros((pos.shape[0], LANES), np.float64)
    out[:, :HYENA_EMB] = np.concatenate([t_norm[:, None], np.cos(ang), -np.sin(ang)], axis=-1)
    return out, t_norm[:, None]


_BLOCK_FILTER_SOURCES = (
    ((2, 1), (1, 1)),
    ((0, 0), (0, 1)),
    ((1, 0), (2, 0)),
)


def _filter_kernel(emb_ref, tn_ref, w1_ref, b1_ref, w2_ref, b2_ref, w3_ref, b3_ref, fr_ref, wo_ref, dl_ref,
                   ce_ref, se_ref, co_ref, so_ref, k1r_ref, k1i_ref, k2r_ref, k2i_ref, h_ref):
    fr = fr_ref[...]
    n_chunks = HYENA_WIDTH // LANES
    for ps in range(3):
        rows = slice(ps * HY_T, (ps + 1) * HY_T)
        h = jnp.sin(fr * (_dot(emb_ref[rows, :].astype(BF16), w1_ref[...]) + b1_ref[...]))
        h = jnp.sin(fr * (_dot(h.astype(BF16), w2_ref[...]) + b2_ref[...]))
        h = jnp.sin(fr * (_dot(h.astype(BF16), w3_ref[...]) + b3_ref[...]))
        ho = _dot(h.astype(BF16), wo_ref[...])
        decay = jnp.exp(-tn_ref[rows, :] * jnp.abs(dl_ref[...]))
        hm = ho * (decay + HYENA_MOD_SHIFT)
        for direction in range(2):
            for c in range(n_chunks):
                c0 = direction * HYENA_WIDTH + c * LANES
                h_ref[2 * ps + direction, c] = hm[:, c0:c0 + LANES]

    first = lax.broadcasted_iota(jnp.int32, (DFT_H, 1), 0) == 0

    def parity(src, par):
        ps, direction = src
        return jnp.concatenate([h_ref[2 * ps + direction, c, pl.ds(par, DFT_H, stride=2), :]
                                for c in range(n_chunks)], axis=1)

    for fi, (fwd_src, bwd_src) in enumerate(_BLOCK_FILTER_SOURCES):
        parts = []
        for par, (cos_ref, sin_ref) in enumerate(((ce_ref, se_ref), (co_ref, so_ref))):
            fwd = parity(fwd_src, par)
            bwd = parity(bwd_src, par)
            if par == 0:
                bwd = jnp.where(first, 0.0, bwd)
            parts.append((_dot(cos_ref[...], (fwd + bwd).astype(BF16)), _dot(sin_ref[...], (bwd - fwd).astype(BF16))))
        (even_re, even_im), (odd_re, odd_im) = parts
        k1r_ref[fi] = even_re + odd_re
        k1i_ref[fi] = even_im + odd_im
        k2r_ref[fi] = even_re - odd_re
        k2i_ref[fi] = even_im - odd_im


def hyena_filter(emb, tnorm, w1, b1, w2, b2, w3, b3, freq, w_out, deltas, fwd_mats):
    spec = jax.ShapeDtypeStruct((HY_FILTERS, DFT_FP, HYENA_WIDTH), F32)
    return pl.pallas_call(
        _filter_kernel,
        out_shape=(spec,) * 4,
        scratch_shapes=[pltpu.VMEM((6, HYENA_WIDTH // LANES, HY_T, LANES), F32)],
        compiler_params=pltpu.CompilerParams(vmem_limit_bytes=VMEM_LIMIT_BYTES),
        name="hyena_filter",
    )(emb, tnorm, w1, b1, w2, b2, w3, b3, freq, w_out, deltas, *fwd_mats)


def _hyena_conv_kernel(z_ref, x0_ref, hb_ref, k1r_ref, k1i_ref, k2r_ref, k2i_ref,
                       ce_ref, se_ref, co_ref, so_ref, ice_ref, ise_ref, ico_ref, iso_ref, o_ref, y_ref):
    n_chunks = MXU_N // LANES
    for cc in range(HYENA_WIDTH // MXU_N):
        _hyena_conv_half(cc, n_chunks, z_ref, x0_ref, hb_ref, k1r_ref, k1i_ref, k2r_ref, k2i_ref,
                         ce_ref, se_ref, co_ref, so_ref, ice_ref, ise_ref, ico_ref, iso_ref, o_ref, y_ref)


def _hyena_conv_half(cc, n_chunks, z_ref, x0_ref, hb_ref, k1r_ref, k1i_ref, k2r_ref, k2i_ref,
                     ce_ref, se_ref, co_ref, so_ref, ice_ref, ise_ref, ico_ref, iso_ref, o_ref, y_ref):
    ch = slice(cc * MXU_N, (cc + 1) * MXU_N)

    def samples(par):
        return jnp.concatenate([z_ref[0, cc * n_chunks + c, pl.ds(j * HY_T + par, DFT_H, stride=2), :]
                                for j in range(HY_BLOCKS) for c in range(n_chunks)], axis=1).astype(BF16)

    ze = samples(0)
    zo = samples(1)
    er = _dot(ce_ref[...], ze)
    ei = _dot(se_ref[...], ze)
    orr = _dot(co_ref[...], zo)
    oi = _dot(so_ref[...], zo)
    x1r, x1i, x2r, x2i = er + orr, ei + oi, er - orr, ei - oi
    a_r, a_i, d_r, d_i = [], [], [], []
    for i in range(HY_BLOCKS):
        y1r = y1i = y2r = y2i = 0.0
        for j in range(HY_BLOCKS):
            lanes = slice(j * MXU_N, (j + 1) * MXU_N)
            fi = i - j + HY_BLOCKS - 1
            k1r, k1i, k2r, k2i = k1r_ref[fi, :, ch], k1i_ref[fi, :, ch], k2r_ref[fi, :, ch], k2i_ref[fi, :, ch]
            y1r = y1r + (x1r[:, lanes] * k1r + x1i[:, lanes] * k1i)
            y1i = y1i + (x1r[:, lanes] * k1i - x1i[:, lanes] * k1r)
            y2r = y2r + (x2r[:, lanes] * k2r + x2i[:, lanes] * k2i)
            y2i = y2i + (x2r[:, lanes] * k2i - x2i[:, lanes] * k2r)
        a_r.append(y1r + y2r)
        a_i.append(y1i + y2i)
        d_r.append(y1r - y2r)
        d_i.append(y1i - y2i)
    a_r, a_i, d_r, d_i = [jnp.concatenate(v, axis=1) for v in (a_r, a_i, d_r, d_i)]
    trow = lax.broadcasted_iota(jnp.int32, (DFT_H, 1), 0)
    sign = jnp.where((trow & 1) == 0, 1.0 / DFT_N, -1.0 / DFT_N)
    ye = (_dot(ice_ref[...], a_r[:DFT_H].astype(BF16)) + _dot(ise_ref[...], a_i[:DFT_H].astype(BF16))
          + sign * a_r[DFT_H:DFT_H + 1])
    yo = (_dot(ico_ref[...], d_r[:DFT_H].astype(BF16)) + _dot(iso_ref[...], d_i[:DFT_H].astype(BF16))
          - sign * d_i[DFT_H:DFT_H + 1])
    for i in range(HY_BLOCKS):
        for c in range(n_chunks):
            lanes = slice(i * MXU_N + c * LANES, i * MXU_N + (c + 1) * LANES)
            y_ref[c, pl.ds(i * HY_T, DFT_H, stride=2), :] = ye[:, lanes]
            y_ref[c, pl.ds(i * HY_T + 1, DFT_H, stride=2), :] = yo[:, lanes]
    for c in range(n_chunks):
        lanes = slice(cc * MXU_N + c * LANES, cc * MXU_N + (c + 1) * LANES)
        y = y_ref[c] + z_ref[0, cc * n_chunks + c] * hb_ref[:, lanes]
        o_ref[0, :, lanes] = (y * x0_ref[0, :, lanes]).astype(BF16)


def hyena_conv(z, x0, h_bias, spectra, fwd_mats, inv_mats):
    b = z.shape[0]
    return pl.pallas_call(
        _hyena_conv_kernel,
        out_shape=jax.ShapeDtypeStruct((b, SEQ, HYENA_WIDTH), BF16),
        grid=(b,),
        in_specs=[
            pl.BlockSpec((1, HYENA_WIDTH // LANES, SEQ, LANES), lambda i: (i, 0, 0, 0)),
            pl.BlockSpec((1, SEQ, HYENA_WIDTH), lambda i: (i, 0, 0)),
            _resident((1, HYENA_WIDTH)),
        ] + [_resident(m.shape) for m in list(spectra) + fwd_mats + inv_mats],
        out_specs=pl.BlockSpec((1, SEQ, HYENA_WIDTH), lambda i: (i, 0, 0)),
        scratch_shapes=[pltpu.VMEM((MXU_N // LANES, SEQ, LANES), F32)],
        compiler_params=_params(),
        name="hyena_conv",
    )(z, x0, h_bias, *spectra, *fwd_mats, *inv_mats)


def _head_rms(t, g):
    low = lax.broadcasted_iota(jnp.int32, (1, LANES), 1) < HEAD_DIM
    blocks = []
    for j in range(D_MODEL // LANES):
        blk = t[:, j * LANES:(j + 1) * LANES]
        sq = blk * blk
        s_lo = jnp.sum(jnp.where(low, sq, 0.0), axis=-1, keepdims=True)
        s_hi = jnp.sum(jnp.where(low, 0.0, sq), axis=-1, keepdims=True)
        ms = jnp.where(low, s_lo, s_hi) * (1.0 / HEAD_DIM)
        blocks.append(blk * lax.rsqrt(ms + EPS))
    return jnp.concatenate(blocks, axis=1) * g


def _qkv_kernel(x_ref, g_ref, w_ref, gq_ref, gk_ref, q_ref, k_ref, v_ref):
    for sub in range(QKV_SUB):
        rows = slice(sub * FFN_TOKENS, (sub + 1) * FFN_TOKENS)
        xn = _rms_scale(x_ref[rows, :], g_ref[...]).astype(BF16)
        q = _dot(xn, w_ref[:, 0:D_MODEL])
        q_ref[rows, :] = (_head_rms(q, gq_ref[...]) * ATT_Q_SCALE).astype(BF16)
        k = _dot(xn, w_ref[:, D_MODEL:2 * D_MODEL])
        k_ref[rows, :] = _head_rms(k, gk_ref[...]).astype(BF16)
        v_ref[rows, :] = _dot(xn, w_ref[:, 2 * D_MODEL:3 * D_MODEL]).astype(BF16)


def qkv_proj(x, g, w_qkv, gq, gk):
    n = x.shape[0]
    step = QKV_SUB * FFN_TOKENS
    tile = pl.BlockSpec((step, D_MODEL), lambda i: (i, 0))
    out = jax.ShapeDtypeStruct((n, D_MODEL), BF16)
    return pl.pallas_call(
        _qkv_kernel,
        out_shape=(out, out, out),
        grid=(n // step,),
        in_specs=[tile, _resident((1, D_MODEL)), _resident(w_qkv.shape), _resident((1, D_MODEL)),
                  _resident((1, D_MODEL))],
        out_specs=(tile, tile, tile),
        compiler_params=_params(),
        name="qkv_proj",
    )(x, g, w_qkv, gq, gk)


def _att_group_start(gi):
    return min(max(ATT_QROWS * gi - NA_ROWS // 2, 0), ROWS - ATT_KROWS)


def _att_group_type(gi):
    if gi < 2:
        return gi
    if gi >= ATT_GROUPS - 2:
        return ATT_TYPES - (ATT_GROUPS - gi)
    return 2


def _att_row_offsets():
    dr = np.full((ATT_TYPES, ATT_QROWS, ATT_KROWS), ATT_DR_INVALID, np.int32)
    reps = {0: 0, 1: 1, 2: 2, 3: ATT_GROUPS - 2, 4: ATT_GROUPS - 1}
    for ty, gi in reps.items():
        start = _att_group_start(gi)
        for j in range(ATT_QROWS):
            qrow = ATT_QROWS * gi + j
            s_q = min(max(qrow - NA_ROWS // 2, 0), ROWS - NA_ROWS)
            for i in range(ATT_KROWS):
                krow = start + i
                if s_q <= krow < s_q + NA_ROWS:
                    dr[ty, j, i] = krow - qrow + NA_ROWS - 1
    return dr


def _attn_kernel(q_ref, k_ref, v_ref, tab_ref, o_ref, bias_ref, vt_ref, s_ref, p_ref):
    low = lax.broadcasted_iota(jnp.int32, (1, LANES), 1) < HEAD_DIM
    dr = _att_row_offsets()

    @pl.when(pl.program_id(1) == 0)
    def _build_bias():
        for hp in range(ATT_PAIRS):
            for ty in range(ATT_TYPES):
                for i in range(ATT_KROWS):
                    for hh in range(2):
                        h = 2 * hp + hh
                        tile = jnp.where(low, tab_ref[h, int(dr[ty, 0, i])], tab_ref[h, int(dr[ty, 1, i])])
                        bias_ref[hp, ty, i * GRID_W:(i + 1) * GRID_W, hh * LANES:(hh + 1) * LANES] = tile

    for hp in range(ATT_PAIRS):
        vt_ref[hp, 0:LANES, :] = v_ref[0, :, hp * LANES:(hp + 1) * LANES].T
        vt_ref[hp, LANES:LANES + ATT_ONES, :] = jnp.ones((ATT_ONES, SEQ), BF16)

    def scores(hp, gi, slot):
        q0 = gi * ATT_Q
        k0 = _att_group_start(gi) * GRID_W
        lanes = slice(hp * LANES, (hp + 1) * LANES)
        qg = q_ref[0, q0:q0 + ATT_Q, lanes]
        zero = jnp.zeros_like(qg)
        qs = jnp.concatenate([jnp.where(low, qg, zero), jnp.where(low, zero, qg)], axis=0)
        st = lax.dot_general(k_ref[0, k0:k0 + ATT_K, lanes], qs, (((1,), (1,)), ((), ())),
                             preferred_element_type=F32)
        s_ref[slot] = st + bias_ref[hp, _att_group_type(gi)]

    def softmax(hp, gi, slot):
        s = s_ref[slot]
        m = jnp.max(s, axis=0, keepdims=True)
        p_ref[slot] = jnp.exp2(s - m).astype(BF16)

    def values(hp, gi, slot):
        q0 = gi * ATT_Q
        k0 = _att_group_start(gi) * GRID_W
        r = _dot(vt_ref[hp, :, k0:k0 + ATT_K], p_ref[slot])
        den = r[LANES:LANES + 1]
        o0 = r[0:HEAD_DIM, 0:LANES] / den[:, 0:LANES]
        o1 = r[HEAD_DIM:LANES, LANES:2 * LANES] / den[:, LANES:2 * LANES]
        o_ref[0, q0:q0 + ATT_Q, hp * LANES:(hp + 1) * LANES] = jnp.concatenate([o0, o1], axis=0).T.astype(BF16)

    items = [(hp, gi) for hp in range(ATT_PAIRS) for gi in range(ATT_GROUPS)]
    for step in range(len(items) + 2 * ATT_STAGE_LAG):
        for stage, fn in enumerate((scores, softmax, values)):
            n = step - stage * ATT_STAGE_LAG
            if 0 <= n < len(items):
                fn(*items[n], n % ATT_GROUPS)


def attention(q, k, v, table):
    b = q.shape[0]
    width = ATT_PAIRS * LANES
    blk = pl.BlockSpec((1, SEQ, width), lambda h, i: (i, 0, h))
    return pl.pallas_call(
        _attn_kernel,
        out_shape=jax.ShapeDtypeStruct((b, SEQ, D_MODEL), BF16),
        grid=(D_MODEL // width, b),
        in_specs=[blk, blk, blk,
                  pl.BlockSpec((2 * ATT_PAIRS, ATT_DR_INVALID + 1, GRID_W, LANES), lambda h, i: (h, 0, 0, 0))],
        out_specs=blk,
        scratch_shapes=[pltpu.VMEM((ATT_PAIRS, ATT_TYPES, ATT_K, 2 * ATT_Q), F32),
                        pltpu.VMEM((ATT_PAIRS, LANES + ATT_ONES, SEQ), BF16),
                        pltpu.VMEM((ATT_GROUPS, ATT_K, 2 * ATT_Q), F32),
                        pltpu.VMEM((ATT_GROUPS, ATT_K, 2 * ATT_Q), BF16)],
        compiler_params=pltpu.CompilerParams(dimension_semantics=("arbitrary", "arbitrary"),
                                             vmem_limit_bytes=VMEM_LIMIT_BYTES),
        name="attention",
    )(q, k, v, table)


def _attn_bias_table(rpb):
    c = np.arange(GRID_W)
    c_start = np.clip(c - NA_COLS // 2, 0, GRID_W - NA_COLS)
    col_ok = (c[None, :] >= c_start[:, None]) & (c[None, :] < c_start[:, None] + NA_COLS)
    dc = np.clip(c[None, :] - c[:, None] + NA_COLS - 1, 0, 2 * NA_COLS - 2)
    onehot = (dc.T[None] == np.arange(2 * NA_COLS - 1)[:, None, None]).astype(np.float32)
    tab = jnp.einsum("hrd,dkq->hrkq", rpb.astype(F32), jnp.asarray(onehot), precision=lax.Precision.HIGHEST)
    tab = jnp.where(jnp.asarray(col_ok.T)[None, None], tab * LOG2_E, NEG_BIG)
    tab = jnp.concatenate([tab, jnp.full((N_HEADS, 1, GRID_W, GRID_W), NEG_BIG, F32)], axis=1)
    return jnp.concatenate([tab, tab], axis=-1)


def _block_diag_pairs(pool_w):
    z = jnp.zeros((LANES, LANES), pool_w.dtype)
    pairs = []
    for pp in range(2):
        top = jnp.concatenate([pool_w[2 * pp], z], axis=1)
        bot = jnp.concatenate([z, pool_w[2 * pp + 1]], axis=1)
        pairs.append(jnp.concatenate([top, bot], axis=0))
    return jnp.stack(pairs)


def kernel(x_prompt, x_sample, norm_g, ffn_w_gate, ffn_w_up, ffn_w_down, w_in_ab, pool_w, pool_scale, short_w, short_b, filt_w1, filt_b1, filt_w2, filt_b2, filt_w3, filt_b3, filt_freq, filt_w_out, filt_deltas, hyena_bias, w_out_ab, w_qkv, q_norm_g, k_norm_g, rpb, w_o):
    wg = ffn_w_gate.astype(BF16)
    wu = ffn_w_up.astype(BF16)
    wd = ffn_w_down.astype(BF16)

    def run_ffn(t, layer, slot, pre=()):
        return ffn(t, norm_g[layer, 2 * slot][None], wg, wu, wd, layer, slot, pre)

    w_in = w_in_ab[0].astype(BF16)
    pool_w2 = _block_diag_pairs(pool_w[0]).astype(BF16)
    w_qkv_b = w_qkv[0].astype(BF16)
    w_o_b = w_o[0].astype(BF16)
    gq = jnp.tile(q_norm_g[0], N_HEADS)[None]
    gk = jnp.tile(k_norm_g[0], N_HEADS)[None]

    fwd_np, inv_np = _dft_matrices()
    fwd_mats = [jnp.asarray(m).astype(BF16) for m in fwd_np]
    inv_mats = [jnp.asarray(m).astype(BF16) for m in inv_np]
    emb_np, tnorm_np = _filter_embedding()
    w1 = jnp.zeros((LANES, HYENA_HIDDEN), BF16).at[:HYENA_EMB].set(filt_w1[0].astype(BF16))
    spectra = hyena_filter(jnp.asarray(emb_np).astype(F32), jnp.asarray(tnorm_np).astype(F32), w1, filt_b1[0][None],
                           filt_w2[0].astype(BF16), filt_b2[0][None], filt_w3[0].astype(BF16), filt_b3[0][None],
                           filt_freq[0][None], filt_w_out[0].astype(BF16), filt_deltas[0][None], fwd_mats)
    table = _attn_bias_table(rpb[0])
    w_out_a = w_out_ab[0, :POOL_WIDTH].astype(BF16)
    w_out_h = w_out_ab[0, POOL_WIDTH:].astype(BF16)

    outs = []
    for x in (x_prompt, x_sample):
        b = x.shape[0]
        n = b * SEQ
        t = run_ffn(x.reshape(n, D_MODEL), 0, 0)
        a, z, x0 = mix0_in(t.reshape(b, SEQ, D_MODEL), norm_g[0, 1][None], w_in, pool_w2, pool_scale[0][None],
                           short_w[0], short_b[0][None])
        yh = hyena_conv(z, x0, hyena_bias[0][None], spectra, fwd_mats, inv_mats)
        t = run_ffn(t, 0, 1, ((a.reshape(n, POOL_WIDTH), w_out_a), (yh.reshape(n, HYENA_WIDTH), w_out_h)))
        t = run_ffn(t, 1, 0)
        q, k, v = qkv_proj(t, norm_g[1, 1][None], w_qkv_b, gq, gk)
        o = attention(q.reshape(b, SEQ, D_MODEL), k.reshape(b, SEQ, D_MODEL), v.reshape(b, SEQ, D_MODEL), table)
        t = run_ffn(t, 1, 1, ((o.reshape(n, D_MODEL), w_o_b),))
        outs.append(t.reshape(b, SEQ, D_MODEL))
    return tuple(outs)
```
